```python
import math
import jax
import jax.numpy as jnp
from jax import lax
import numpy as np

D_MODEL = 1024
BATCH = 8
SEQ = 2048
DEPTH = 4

GRID_W = 64
CTX_LEN = 256
NORM_EPS = 1e-6
N_MOD = 6

D_MIX = D_MODEL
D_HYENA = D_MIX // 4
HYENA_SHORT = 3
HYENA_BANDS = 16
HYENA_EMB = 1 + 2 * HYENA_BANDS
HYENA_FILTER_HIDDEN = 64
HYENA_DECAY_TARGET = 1e-2
HYENA_FAST_PCT = 0.3
HYENA_SLOW_PCT = 1.5
RWKV_HEAD_DIM = 64
D_RWKV = D_MIX // 4
RWKV_HEADS = D_RWKV // RWKV_HEAD_DIM
RWKV_DECAY_LORA = 64
RWKV_ICLR_LORA = 64
RWKV_GATE_LORA = 128
RWKV_GN_EPS = 64e-5
GDN_HEAD_DIM = 128
D_GDN = D_MIX - D_HYENA - D_RWKV
GDN_HEADS = D_GDN // GDN_HEAD_DIM
GDN_CONV = 5
GDN_CHUNK = 64
HY_COLS = 3 * D_HYENA
RW_COLS = 3 * D_RWKV + 2 * RWKV_DECAY_LORA + 2 * RWKV_ICLR_LORA + RWKV_GATE_LORA
GDN_COLS = 4 * D_GDN + 4 * GDN_HEADS
IN_COLS = HY_COLS + RW_COLS + GDN_COLS
N_EXPERTS = 32
TOP_K = 4
D_EXPERT = D_MODEL
SWIGLU_ALPHA = 1.702
SWIGLU_LIMIT = 7.0
MOE_BLOCK = 256

kernel_name = 'hybrid_hyena_rwkv7_gdn_moe_dit'


def _rev(t):
    return jnp.flip(t, axis=1)


def rms_norm(h, w):
    hf = h.astype(jnp.float32)
    hf = hf * lax.rsqrt(jnp.mean(hf * hf, axis=-1, keepdims=True) + NORM_EPS)
    return (hf * w.astype(jnp.float32)).astype(h.dtype)


def modulate(h, shift, scale):
    return h * (1 + scale) + shift


def l2_normalize(t):
    return t * lax.rsqrt(jnp.sum(t * t, axis=-1, keepdims=True) + 1e-12)


def dwconv_centred(u, w):
    K, C = w.shape
    return lax.conv_general_dilated(u, w[:, None, :].astype(u.dtype), window_strides=(1,),
                                    padding=[(K // 2, K // 2)],
                                    dimension_numbers=('NWC', 'WIO', 'NWC'),
                                    feature_group_count=C)


def split_cols(p):
    return p[..., :HY_COLS], p[..., HY_COLS:HY_COLS + RW_COLS], p[..., HY_COLS + RW_COLS:]


def hyena_filters(L, w1, b1, w2, b2, w3, b3, w4, freq):
    f32 = jnp.float32
    pos = jnp.arange(L, dtype=f32)
    t = jnp.linspace(0.0, 1.0, L, dtype=f32)[:, None]
    bands = jnp.linspace(1e-4, HYENA_BANDS - 1, HYENA_BANDS, dtype=f32)
    ang = (2.0 * math.pi / L) * pos[:, None] * bands
    z = jnp.concatenate([t, jnp.cos(ang), -jnp.sin(ang)], axis=-1)
    fr = freq.astype(f32)
    hdn = jnp.sin(fr * (z @ w1.astype(f32) + b1.astype(f32)))
    hdn = jnp.sin(fr * (hdn @ w2.astype(f32) + b2.astype(f32)))
    hdn = jnp.sin(fr * (hdn @ w3.astype(f32) + b3.astype(f32)))
    h = hdn @ w4.astype(f32)
    deltas = jnp.linspace(math.log(HYENA_DECAY_TARGET) / HYENA_SLOW_PCT,
                          math.log(HYENA_DECAY_TARGET) / HYENA_FAST_PCT, D_HYENA, dtype=f32)
    window = jnp.exp(-t * jnp.abs(deltas))
    h_fwd = h[:, :D_HYENA] * window
    h_bwd = h[:, D_HYENA:] * window
    return jnp.concatenate([h_fwd, jnp.zeros((1, D_HYENA), f32), h_bwd[:0:-1]], axis=0)


def hyena_mixer(u, short_w, short_b, w1, b1, w2, b2, w3, b3, w4, freq, skip):
    L = u.shape[1]
    uc = dwconv_centred(u, short_w) + short_b
    x0, x1, v = jnp.split(uc, 3, axis=-1)
    v = (v * x1).astype(jnp.float32)
    k = hyena_filters(L, w1, b1, w2, b2, w3, b3, w4, freq)
    vf = jnp.fft.rfft(v, n=2 * L, axis=1)
    kf = jnp.fft.rfft(k, n=2 * L, axis=0)
    y = jnp.fft.irfft(vf * kf[None], n=2 * L, axis=1)[:, :L] + v * skip.astype(jnp.float32)
    return y.astype(u.dtype) * x0


def token_shift_grid(p):
    B, L, C = p.shape
    rows = L // GRID_W
    g = p.reshape(B, rows, GRID_W, C // 4, 4)
    gp = jnp.pad(g, ((0, 0), (1, 1), (1, 1), (0, 0), (0, 0)))
    left = gp[:, 1:-1, :-2, :, 0]
    right = gp[:, 1:-1, 2:, :, 1]
    up = gp[:, :-2, 1:-1, :, 2]
    down = gp[:, 2:, 1:-1, :, 3]
    return jnp.stack([left, right, up, down], axis=-1).reshape(B, L, C)


def token_shift_seq(p):
    B, L, C = p.shape
    s = jnp.pad(p.reshape(B, L, C // 2, 2), ((0, 0), (1, 1), (0, 0), (0, 0)))
    return jnp.stack([s[:, :-2, :, 0], s[:, 2:, :, 1]], axis=-1).reshape(B, L, C)


def rwkv7_prepare(m, w0, w2, a0, a2, g2, k_k, k_a):
    m = m.astype(jnp.float32)
    B, L, _ = m.shape
    H, N = RWKV_HEADS, RWKV_HEAD_DIM
    o1, o2, o3 = D_RWKV, 2 * D_RWKV, 3 * D_RWKV
    o4 = o3 + 2 * RWKV_DECAY_LORA
    o5 = o4 + 2 * RWKV_ICLR_LORA
    r, k, v = m[..., :o1], m[..., o1:o2], m[..., o2:o3]
    wd = m[..., o3:o4].reshape(B, L, 2, RWKV_DECAY_LORA)
    ad = m[..., o4:o5].reshape(B, L, 2, RWKV_ICLR_LORA)
    gd = m[..., o5:]
    lw = -jax.nn.softplus(-(w0 + jnp.einsum('bldr,drc->bldc', jnp.tanh(wd), w2))) - 0.5
    decay = jnp.exp(-jnp.exp(lw))
    a = jax.nn.sigmoid(a0 + jnp.einsum('bldr,drc->bldc', ad, a2))
    g = jax.nn.sigmoid(gd) @ g2
    kk = l2_normalize((k * k_k).reshape(B, L, H, N))
    k_dir = k[:, :, None] * (1 + (a - 1) * k_a)
    heads = lambda t: t.reshape(t.shape[:-1] + (H, N))
    return heads(r), heads(k_dir), heads(v), kk, heads(decay), heads(a), g


def rwkv7_dir(inp, d):
    r, k_dir, v, kk, decay, a, _ = inp
    seq = (r, decay[:, :, d], k_dir[:, :, d], v, -kk, kk * a[:, :, d])
    return tuple(_rev(t) for t in seq) if d else seq


def rwkv7_scan(s0, r, w, k, v, a, b):
    def step(s, inp):
        r_t, w_t, k_t, v_t, a_t, b_t = inp
        sa = jnp.einsum('bhvk,bhk->bhv', s, a_t)
        s = s * w_t[:, :, None, :] + sa[..., None] * b_t[:, :, None, :] + v_t[..., None] * k_t[:, :, None, :]
        return s, jnp.einsum('bhvk,bhk->bhv', s, r_t)
    xs = tuple(jnp.swapaxes(t, 0, 1) for t in (r, w, k, v, a, b))
    s, o = lax.scan(step, s0, xs)
    return s, jnp.swapaxes(o, 0, 1)


def rwkv7_finish(o, inp, r_k, gn_w, gn_b):
    r, k_dir, v, kk, decay, a, g = inp
    B, L, H, N = o.shape
    mean = jnp.mean(o, axis=-1, keepdims=True)
    var = jnp.mean(jnp.square(o - mean), axis=-1, keepdims=True)
    y = ((o - mean) * lax.rsqrt(var + RWKV_GN_EPS)).reshape(B, L, H * N) * gn_w + gn_b
    bonus = jnp.sum(r[:, :, None] * k_dir * r_k.reshape(H, N), axis=(2, 4))
    y = y + (bonus[..., None] * v).reshape(B, L, H * N)
    return y * g


def rwkv7_mixer(p_ctx, p_lat, mu, w0, w2, a0, a2, g2, k_k, k_a, r_k, gn_w, gn_b, with_ctx):
    B = p_lat.shape[0]
    m_ctx = p_ctx + mu * (token_shift_seq(p_ctx) - p_ctx)
    m_lat = p_lat + mu * (token_shift_grid(p_lat) - p_lat)
    ctx_in = rwkv7_prepare(m_ctx, w0, w2, a0, a2, g2, k_k, k_a)
    lat_in = rwkv7_prepare(m_lat, w0, w2, a0, a2, g2, k_k, k_a)
    s0 = jnp.zeros((B, RWKV_HEADS, RWKV_HEAD_DIM, RWKV_HEAD_DIM), jnp.float32)
    o_ctx, o_lat = 0.0, 0.0
    for d in range(2):
        s_c, oc = rwkv7_scan(s0, *rwkv7_dir(ctx_in, d))
        _, ol = rwkv7_scan(s_c, *rwkv7_dir(lat_in, d))
        o_ctx = o_ctx + (_rev(oc) if d else oc)
        o_lat = o_lat + (_rev(ol) if d else ol)
    y_lat = rwkv7_finish(o_lat, lat_in, r_k, gn_w, gn_b).astype(p_lat.dtype)
    y_ctx = rwkv7_finish(o_ctx, ctx_in, r_k, gn_w, gn_b).astype(p_ctx.dtype) if with_ctx else None
    return y_ctx, y_lat


def gdn_prepare(p, conv_w, a_log, dt_bias):
    B, L, _ = p.shape
    H, K = GDN_HEADS, GDN_HEAD_DIM
    qkv = jax.nn.silu(dwconv_centred(p[..., :3 * D_GDN], conv_w)).astype(jnp.float32)
    q, k, v = [t.reshape(B, L, H, K) for t in jnp.split(qkv, 3, axis=-1)]
    z = p[..., 3 * D_GDN:4 * D_GDN]
    rest = p[..., 4 * D_GDN:].astype(jnp.float32)
    beta = jax.nn.sigmoid(rest[..., :2 * H].reshape(B, L, 2, H))
    g = -jnp.exp(a_log) * jax.nn.softplus(rest[..., 2 * H:].reshape(B, L, 2, H) + dt_bias)
    return l2_normalize(q), l2_normalize(k), v, z, beta, g


def gdn_chunk_scan(q, k, v, g, beta, s0):
    B, L, H, K = q.shape
    V = v.shape[-1]
    C = GDN_CHUNK
    n = L // C
    def chunks(t):
        return jnp.moveaxis(t.reshape((B, n, C, H) + t.shape[3:]), 3, 1)
    q = chunks(q * (K ** -0.5))
    k, v, g, beta = chunks(k), chunks(v), chunks(g), chunks(beta)
    gc = jnp.cumsum(g, axis=-1)
    causal = jnp.tril(jnp.ones((C, C), bool))
    strict = jnp.tril(jnp.ones((C, C), bool), -1)
    diff = gc[..., :, None] - gc[..., None, :]
    decay = jnp.where(causal, jnp.exp(jnp.where(causal, diff, 0.0)), 0.0)
    kb = k * beta[..., None]
    m = jnp.where(strict, jnp.einsum('bhnik,bhnjk->bhnij', kb, k) * decay, 0.0)
    rhs = jnp.concatenate([v * beta[..., None], kb * jnp.exp(gc)[..., None]], axis=-1)
    sol = lax.linalg.triangular_solve(m + jnp.eye(C, dtype=jnp.float32), rhs, left_side=True,
                                      lower=True, unit_diagonal=True)
    u, w = sol[..., :V], sol[..., V:]
    attn = jnp.einsum('bhnik,bhnjk->bhnij', q, k) * decay
    def step(s, xs):
        q_i, k_i, u_i, w_i, g_i, a_i = xs
        v_new = u_i - jnp.einsum('bhck,bhkv->bhcv', w_i, s)
        o = (jnp.einsum('bhck,bhkv->bhcv', q_i * jnp.exp(g_i)[..., None], s)
             + jnp.einsum('bhij,bhjv->bhiv', a_i, v_new))
        g_last = g_i[..., -1:]
        s = (s * jnp.exp(g_last)[..., None]
             + jnp.einsum('bhck,bhcv->bhkv', k_i * jnp.exp(g_last - g_i)[..., None], v_new))
        return s, o
    xs = tuple(jnp.moveaxis(t, 2, 0) for t in (q, k, u, w, gc, attn))
    s, o = lax.scan(step, s0, xs)
    return s, jnp.transpose(o, (1, 0, 3, 2, 4)).reshape(B, L, H, V)


def gdn_finish(o, z, norm_w):
    B, L, H, V = o.shape
    o = o * lax.rsqrt(jnp.mean(o * o, axis=-1, keepdims=True) + NORM_EPS) * norm_w.astype(jnp.float32)
    y = o * jax.nn.silu(z.astype(jnp.float32)).reshape(B, L, H, V)
    return y.reshape(B, L, H * V).astype(z.dtype)


def gdn_mixer(p_ctx, p_lat, conv_w, a_log, dt_bias, norm_w, with_ctx):
    B = p_lat.shape[0]
    qc, kc, vc, zc, bc, gc = gdn_prepare(p_ctx, conv_w, a_log, dt_bias)
    ql, kl, vl, zl, bl, gl = gdn_prepare(p_lat, conv_w, a_log, dt_bias)
    s0 = jnp.zeros((B, GDN_HEADS, GDN_HEAD_DIM, GDN_HEAD_DIM), jnp.float32)
    o_ctx, o_lat = 0.0, 0.0
    for d in range(2):
        f = _rev if d else (lambda t: t)
        s_c, oc = gdn_chunk_scan(f(qc), f(kc), f(vc), f(gc[:, :, d]), f(bc[:, :, d]), s0)
        _, ol = gdn_chunk_scan(f(ql), f(kl), f(vl), f(gl[:, :, d]), f(bl[:, :, d]), s_c)
        o_ctx = o_ctx + f(oc)
        o_lat = o_lat + f(ol)
    y_lat = gdn_finish(o_lat, zl, norm_w)
    y_ctx = gdn_finish(o_ctx, zc, norm_w) if with_ctx else None
    return y_ctx, y_lat


def moe_ffn(h, router_w, router_b, w_gu, b_gu, w_dn, b_dn):
    N, D = h.shape
    logits = (h @ router_w + router_b).astype(jnp.float32)
    top_logit, top_idx = lax.top_k(logits, TOP_K)
    gates = jax.nn.softmax(top_logit, axis=-1).astype(h.dtype)
    A = N * TOP_K
    flat_e = top_idx.reshape(A)
    flat_tok = jnp.arange(A, dtype=jnp.int32) // TOP_K
    flat_g = gates.reshape(A)
    order = jnp.argsort(flat_e)
    sorted_e = flat_e[order]
    counts = jnp.bincount(flat_e, length=N_EXPERTS)
    starts = jnp.cumsum(counts) - counts
    blocks_per = (counts + MOE_BLOCK - 1) // MOE_BLOCK
    block_end = jnp.cumsum(blocks_per)
    padded_starts = (block_end - blocks_per) * MOE_BLOCK
    dest = padded_starts[sorted_e] + jnp.arange(A, dtype=jnp.int32) - starts[sorted_e]
    n_blocks = -(-A // MOE_BLOCK) + N_EXPERTS
    slot_tok = jnp.full((n_blocks * MOE_BLOCK,), N, jnp.int32).at[dest].set(flat_tok[order])
    slot_g = jnp.zeros((n_blocks * MOE_BLOCK,), h.dtype).at[dest].set(flat_g[order])
    block_e = jnp.minimum(jnp.searchsorted(block_end, jnp.arange(n_blocks), side='right'), N_EXPERTS - 1)
    h_pad = jnp.concatenate([h, jnp.zeros((1, D), h.dtype)], axis=0)
    xb = h_pad[slot_tok].reshape(n_blocks, MOE_BLOCK, D)
    def expert_block(args):
        xblk, e = args
        gu = xblk @ w_gu[e] + b_gu[e]
        gate = jnp.minimum(gu[:, :D_EXPERT], SWIGLU_LIMIT)
        up = jnp.clip(gu[:, D_EXPERT:], -SWIGLU_LIMIT, SWIGLU_LIMIT)
        glu = gate * jax.nn.sigmoid(gate * SWIGLU_ALPHA)
        return ((up + 1) * glu) @ w_dn[e] + b_dn[e]
    yb = lax.map(expert_block, (xb, block_e))
    y = jax.ops.segment_sum(yb.reshape(-1, D) * slot_g[:, None], slot_tok, num_segments=N + 1)
    return y[:N]


def setup_inputs(seed: int = 0) -> dict:
    key = jax.random.key(seed)
    keys = iter(jax.random.split(key, 64))
    def nrm(shape, scale):
        return jax.random.normal(next(keys), shape, jnp.float32) * scale
    def unif(shape, lo, hi):
        return jax.random.uniform(next(keys), shape, jnp.float32, lo, hi)
    D, E, F, FH = D_MODEL, N_EXPERTS, D_EXPERT, HYENA_FILTER_HIDDEN
    dt = jnp.exp(unif((DEPTH, 2, GDN_HEADS), math.log(1e-3), math.log(1e-1)))
    return {
        'x': nrm((BATCH, SEQ, D), 1.0),
        'c': nrm((BATCH, D), 1.0),
        'ctx': nrm((BATCH, CTX_LEN, D), 1.0),
        'c_ctx': nrm((D,), 1.0),
        'ada_w': nrm((DEPTH, D, N_MOD * D), 0.5 * D ** -0.5),
        'ada_b': nrm((DEPTH, N_MOD * D), 0.02),
        'norm_mix_w': 1.0 + nrm((DEPTH, D), 0.02),
        'norm_ffn_w': 1.0 + nrm((DEPTH, D), 0.02),
        'final_norm_w': 1.0 + nrm((D,), 0.02),
        'w_in': nrm((DEPTH, D, IN_COLS), D ** -0.5),
        'w_out': nrm((DEPTH, D_MIX, D), D_MIX ** -0.5),
        'hy_short_w': nrm((DEPTH, HYENA_SHORT, HY_COLS), HYENA_SHORT ** -0.5),
        'hy_short_b': nrm((DEPTH, HY_COLS), 0.02),
        'hy_f_w1': nrm((DEPTH, HYENA_EMB, FH), HYENA_EMB ** -0.5),
        'hy_f_b1': nrm((DEPTH, FH), 0.02),
        'hy_f_w2': nrm((DEPTH, FH, FH), FH ** -0.5),
        'hy_f_b2': nrm((DEPTH, FH), 0.02),
        'hy_f_w3': nrm((DEPTH, FH, FH), FH ** -0.5),
        'hy_f_b3': nrm((DEPTH, FH), 0.02),
        'hy_f_w4': nrm((DEPTH, FH, 2 * D_HYENA), 0.05 * FH ** -0.5),
        'hy_f_freq': 1.0 + nrm((DEPTH, FH), 0.02),
        'hy_skip': nrm((DEPTH, D_HYENA), 0.5),
        'rw_mu': unif((DEPTH, RW_COLS), 0.0, 1.0),
        'rw_w0': unif((DEPTH, 2, D_RWKV), -6.0, -1.0),
        'rw_w2': nrm((DEPTH, 2, RWKV_DECAY_LORA, D_RWKV), 0.1 * RWKV_DECAY_LORA ** -0.5),
        'rw_a0': nrm((DEPTH, 2, D_RWKV), 0.1),
        'rw_a2': nrm((DEPTH, 2, RWKV_ICLR_LORA, D_RWKV), 0.5 * RWKV_ICLR_LORA ** -0.5),
        'rw_g2': nrm((DEPTH, RWKV_GATE_LORA, D_RWKV), RWKV_GATE_LORA ** -0.5),
        'rw_k_k': 0.85 + nrm((DEPTH, D_RWKV), 0.02),
        'rw_k_a': 1.0 + nrm((DEPTH, D_RWKV), 0.02),
        'rw_r_k': nrm((DEPTH, D_RWKV), 0.1),
        'rw_gn_w': 1.0 + nrm((DEPTH, D_RWKV), 0.02),
        'rw_gn_b': nrm((DEPTH, D_RWKV), 0.01),
        'gdn_conv_w': nrm((DEPTH, GDN_CONV, 3 * D_GDN), GDN_CONV ** -0.5),
        'gdn_a_log': jnp.log(unif((DEPTH, 2, GDN_HEADS), 1.0, 16.0)),
        'gdn_dt_bias': dt + jnp.log(-jnp.expm1(-dt)),
        'gdn_norm_w': 1.0 + nrm((DEPTH, GDN_HEAD_DIM), 0.02),
        'moe_router_w': nrm((DEPTH, D, E), D ** -0.5),
        'moe_router_b': nrm((DEPTH, E), 0.01),
        'moe_w_gu': nrm((DEPTH, E, D, 2 * F), D ** -0.5),
        'moe_b_gu': nrm((DEPTH, E, 2 * F), 0.01),
        'moe_w_dn': nrm((DEPTH, E, F, D), F ** -0.5),
        'moe_b_dn': nrm((DEPTH, E, D), 0.01),
    }


def reference(x, c, ctx, c_ctx, ada_w, ada_b, norm_mix_w, norm_ffn_w, final_norm_w, w_in, w_out,
              hy_short_w, hy_short_b, hy_f_w1, hy_f_b1, hy_f_w2, hy_f_b2, hy_f_w3, hy_f_b3, hy_f_w4,
              hy_f_freq, hy_skip, rw_mu, rw_w0, rw_w2, rw_a0, rw_a2, rw_g2, rw_k_k, rw_k_a, rw_r_k,
              rw_gn_w, rw_gn_b, gdn_conv_w, gdn_a_log, gdn_dt_bias, gdn_norm_w,
              moe_router_w, moe_router_b, moe_w_gu, moe_b_gu, moe_w_dn, moe_b_dn):
    B, L, D = x.shape
    Lc = ctx.shape[1]
    h_lat, h_ctx = x, ctx
    s_lat = jax.nn.silu(c)
    s_ctx = jax.nn.silu(c_ctx)
    for l in range(DEPTH):
        last = l == DEPTH - 1
        ml = jnp.split((s_lat @ ada_w[l] + ada_b[l])[:, None, :], N_MOD, axis=-1)
        mc = jnp.split(s_ctx @ ada_w[l] + ada_b[l], N_MOD, axis=-1)
        n_lat = modulate(rms_norm(h_lat, norm_mix_w[l]), ml[0], ml[1])
        n_ctx = modulate(rms_norm(h_ctx, norm_mix_w[l]), mc[0], mc[1])
        hy_l, rw_l, gd_l = split_cols(n_lat @ w_in[l])
        hy_c, rw_c, gd_c = split_cols(n_ctx @ w_in[l])
        hy_p = (hy_short_w[l], hy_short_b[l], hy_f_w1[l], hy_f_b1[l], hy_f_w2[l], hy_f_b2[l],
                hy_f_w3[l], hy_f_b3[l], hy_f_w4[l], hy_f_freq[l], hy_skip[l])
        y_hy_l = hyena_mixer(hy_l, *hy_p)
        y_rw_c, y_rw_l = rwkv7_mixer(rw_c, rw_l, rw_mu[l], rw_w0[l], rw_w2[l], rw_a0[l], rw_a2[l],
                                     rw_g2[l], rw_k_k[l], rw_k_a[l], rw_r_k[l], rw_gn_w[l], rw_gn_b[l],
                                     not last)
        y_gd_c, y_gd_l = gdn_mixer(gd_c, gd_l, gdn_conv_w[l], gdn_a_log[l], gdn_dt_bias[l],
                                   gdn_norm_w[l], not last)
        h_lat = h_lat + ml[2] * (jnp.concatenate([y_hy_l, y_rw_l, y_gd_l], axis=-1) @ w_out[l])
        moe_p = (moe_router_w[l], moe_router_b[l], moe_w_gu[l], moe_b_gu[l], moe_w_dn[l], moe_b_dn[l])
        if last:
            n_lat = modulate(rms_norm(h_lat, norm_ffn_w[l]), ml[3], ml[4])
            h_lat = h_lat + ml[5] * moe_ffn(n_lat.reshape(B * L, D), *moe_p).reshape(B, L, D)
        else:
            y_hy_c = hyena_mixer(hy_c, *hy_p)
            h_ctx = h_ctx + mc[2] * (jnp.concatenate([y_hy_c, y_rw_c, y_gd_c], axis=-1) @ w_out[l])
            n_lat = modulate(rms_norm(h_lat, norm_ffn_w[l]), ml[3], ml[4])
            n_ctx = modulate(rms_norm(h_ctx, norm_ffn_w[l]), mc[3], mc[4])
            y = moe_ffn(jnp.concatenate([n_ctx.reshape(B * Lc, D), n_lat.reshape(B * L, D)], axis=0), *moe_p)
            h_ctx = h_ctx + mc[5] * y[:B * Lc].reshape(B, Lc, D)
            h_lat = h_lat + ml[5] * y[B * Lc:].reshape(B, L, D)
    return rms_norm(h_lat, final_norm_w)
```

```python
import functools
import math

import jax
import jax.numpy as jnp
import numpy as np
from jax import lax
from jax.experimental import pallas as pl
from jax.experimental.pallas import tpu as pltpu

F32 = jnp.float32
BF16 = jnp.bfloat16

NORM_EPS = 1e-6
N_MOD = 6
GRID_W = 64
CHUNK = 64
HYENA_BANDS = 16
HYENA_DECAY_TARGET = 1e-2
HYENA_FAST_PCT = 0.3
HYENA_SLOW_PCT = 1.5
RWKV_HEAD_DIM = 64
RWKV_GN_EPS = 64e-5
GDN_HEAD_DIM = 128
TOP_K = 4
SWIGLU_ALPHA = 1.702
SWIGLU_LIMIT = 7.0

TOKEN_TILE = 256
MOE_TILE = 256
VMEM_LIMIT = 56 * 1024 * 1024


def _cparams(sem):
    return pltpu.CompilerParams(dimension_semantics=sem, vmem_limit_bytes=VMEM_LIMIT)


def _split_bf16(x):
    hi = x.astype(BF16)
    lo = (x - hi.astype(F32)).astype(BF16)
    return hi, lo


def _dot(a, b):
    return jnp.dot(a, b, preferred_element_type=F32)


def _dot3(a, b):
    ah, al = _split_bf16(a)
    bh, bl = _split_bf16(b)
    return _dot(ah, bh) + _dot(al, bh) + _dot(ah, bl)


def _in_proj_kernel(h_ref, shift_ref, scale_ref, nw_ref, w_ref, hy_ref, rw_ref, gd_ref, *, splits):
    h = h_ref[0]
    n = h * lax.rsqrt(jnp.mean(h * h, axis=-1, keepdims=True) + NORM_EPS) * nw_ref[...]
    n = (n * (1.0 + scale_ref[0, 0]) + shift_ref[0, 0]).astype(BF16)
    c0, c1, c2 = splits
    hy_ref[0] = _dot(n, w_ref[:, :c0])
    rw_ref[0] = _dot(n, w_ref[:, c0:c0 + c1])
    gd_ref[0] = _dot(n, w_ref[:, c0 + c1:c0 + c1 + c2])


def _in_proj(h, shift, scale, norm_w, w_bf, splits, n_ctx_tiles):
    B, T, D = h.shape
    tm = TOKEN_TILE
    seg = lambda b, i: (b, (i >= n_ctx_tiles).astype(jnp.int32), 0, 0)
    cols = w_bf.shape[1]
    outs = [jax.ShapeDtypeStruct((B, T, c), F32) for c in splits]
    return pl.pallas_call(
        functools.partial(_in_proj_kernel, splits=splits),
        out_shape=outs,
        grid=(B, T // tm),
        in_specs=[
            pl.BlockSpec((1, tm, D), lambda b, i: (b, i, 0)),
            pl.BlockSpec((1, 1, 1, D), seg),
            pl.BlockSpec((1, 1, 1, D), seg),
            pl.BlockSpec((1, D), lambda b, i: (0, 0)),
            pl.BlockSpec((D, cols), lambda b, i: (0, 0)),
        ],
        out_specs=[pl.BlockSpec((1, tm, c), lambda b, i: (b, i, 0)) for c in splits],
        compiler_params=_cparams(("parallel", "parallel")),
        name="in_proj",
    )(h, shift, scale, norm_w, w_bf)


def _out_proj_kernel(h_ref, hy_ref, rw_ref, gd_ref, w_ref, gate_ref, shift_ref, scale_ref, nw_ref,
                     rwt_ref, rb_ref, hn_ref, n_ref, lg_ref, *, splits):
    c0, c1, c2 = splits
    acc = _dot(hy_ref[0].astype(BF16), w_ref[:c0, :])
    acc += _dot(rw_ref[0].astype(BF16), w_ref[c0:c0 + c1, :])
    acc += _dot(gd_ref[0].astype(BF16), w_ref[c0 + c1:c0 + c1 + c2, :])
    h = h_ref[0] + gate_ref[0, 0] * acc
    hn_ref[0] = h
    n = h * lax.rsqrt(jnp.mean(h * h, axis=-1, keepdims=True) + NORM_EPS) * nw_ref[...]
    n = n * (1.0 + scale_ref[0, 0]) + shift_ref[0, 0]
    n_ref[0] = n.astype(BF16)
    lg_ref[0] = _dot3(n, rwt_ref[...]) + rb_ref[...]


def _out_proj(h, y_hy, y_rw, y_gd, w_bf, gate, shift, scale, norm_w, router_w, router_b, n_ctx_tiles):
    B, T, D = h.shape
    tm = TOKEN_TILE
    E = router_w.shape[1]
    splits = (y_hy.shape[-1], y_rw.shape[-1], y_gd.shape[-1])
    seg = lambda b, i: (b, (i >= n_ctx_tiles).astype(jnp.int32), 0, 0)
    tok = lambda c: pl.BlockSpec((1, tm, c), lambda b, i: (b, i, 0))
    full = lambda s: pl.BlockSpec(s, lambda b, i: (0,) * len(s))
    return pl.pallas_call(
        functools.partial(_out_proj_kernel, splits=splits),
        out_shape=[jax.ShapeDtypeStruct((B, T, D), F32), jax.ShapeDtypeStruct((B, T, D), BF16),
                   jax.ShapeDtypeStruct((B, T, E), F32)],
        grid=(B, T // tm),
        in_specs=[tok(D), tok(splits[0]), tok(splits[1]), tok(splits[2]), full(w_bf.shape),
                  pl.BlockSpec((1, 1, 1, D), seg), pl.BlockSpec((1, 1, 1, D), seg),
                  pl.BlockSpec((1, 1, 1, D), seg), full((1, D)), full(router_w.shape), full((1, E))],
        out_specs=[tok(D), tok(D), tok(E)],
        compiler_params=_cparams(("parallel", "parallel")),
        name="out_proj",
    )(h, y_hy, y_rw, y_gd, w_bf, gate, shift, scale, norm_w, router_w, router_b)


def _hyena_conv_kernel(v_ref, cm_ref, sm_ref, p_ref, q_ref, pn_ref, sgn_ref, y_ref):
    v = v_ref[0]
    vb = v.astype(BF16)
    a = _dot(cm_ref[...], vb)
    b = _dot(sm_ref[...], vb)
    p = p_ref[...]
    q = q_ref[...]
    yr = (a * p - b * q).astype(BF16)
    yi = (a * q + b * p).astype(BF16)
    y = _dot(cm_ref[...], yr) + _dot(sm_ref[...], yi)
    sgn = sgn_ref[...]
    a_nyq = jnp.sum(v * sgn, axis=0, keepdims=True)
    y_ref[0] = y + sgn * (a_nyq * pn_ref[...])


def _hyena_conv(v, cm, sm, p, q, pn, sgn):
    B, L, Ch = v.shape
    full = lambda s: pl.BlockSpec(s, lambda b: (0,) * len(s), pipeline_mode=pl.Buffered(1))
    return pl.pallas_call(
        _hyena_conv_kernel,
        out_shape=jax.ShapeDtypeStruct((B, L, Ch), F32),
        grid=(B,),
        in_specs=[pl.BlockSpec((1, L, Ch), lambda b: (b, 0, 0)), full((L, L)), full((L, L)),
                  full((L, Ch)), full((L, Ch)), full((1, Ch)), full((L, 1))],
        out_specs=pl.BlockSpec((1, L, Ch), lambda b: (b, 0, 0)),
        compiler_params=_cparams(("parallel",)),
        name="hyena_conv",
    )(v, cm, sm, p, q, pn, sgn)


def _affine_scan_kernel(r_ref, o_ref, m_ref, n_ref, out_ref, s_ref, *, chains):
    @pl.when(pl.program_id(1) == 0)
    def _():
        s_ref[...] = jnp.zeros_like(s_ref)

    for g in range(chains):
        s = s_ref[g]
        out_ref[g, 0] = _dot3(r_ref[g, 0], s) + o_ref[g, 0]
        s_ref[g] = _dot3(m_ref[g, 0], s) + n_ref[g, 0]


def _affine_scan(rh, oloc, mt, nt, chains):
    G, n, C, K = rh.shape
    V = oloc.shape[-1]
    chains = math.gcd(chains, G)
    blk = lambda a, b: pl.BlockSpec((chains, 1, a, b), lambda g, c: (g, c, 0, 0))
    return pl.pallas_call(
        functools.partial(_affine_scan_kernel, chains=chains),
        out_shape=jax.ShapeDtypeStruct((G, n, C, V), F32),
        grid=(G // chains, n),
        in_specs=[blk(C, K), blk(C, V), blk(K, K), blk(K, V)],
        out_specs=blk(C, V),
        scratch_shapes=[pltpu.VMEM((chains, K, V), F32)],
        compiler_params=_cparams(("parallel", "arbitrary")),
        name="affine_scan",
    )(rh, oloc, mt, nt)


def _moe_kernel(be_ref, nu_ref, x_ref, wgu_ref, bgu_ref, wdn_ref, bdn_ref, y_ref, wgu_bf, wdn_bf, *, d_expert):
    i = pl.program_id(0)
    prev = be_ref[jnp.maximum(i - 1, 0)]
    changed = jnp.logical_or(i == 0, be_ref[i] != prev)

    @pl.when(changed)
    def _():
        wgu_bf[...] = wgu_ref[0].astype(BF16)
        wdn_bf[...] = wdn_ref[0].astype(BF16)

    @pl.when(i < nu_ref[0])
    def _():
        gu = _dot(x_ref[...], wgu_bf[...]) + bgu_ref[0]
        gate = jnp.minimum(gu[:, :d_expert], SWIGLU_LIMIT)
        up = jnp.clip(gu[:, d_expert:], -SWIGLU_LIMIT, SWIGLU_LIMIT)
        glu = gate * jax.nn.sigmoid(gate * SWIGLU_ALPHA)
        act = ((up + 1.0) * glu).astype(BF16)
        y_ref[...] = _dot(act, wdn_bf[...]) + bdn_ref[0]

    @pl.when(i >= nu_ref[0])
    def _():
        y_ref[...] = jnp.zeros_like(y_ref)


def _moe_experts(xs, block_e, n_used, w_gu, b_gu, w_dn, b_dn):
    NP, D = xs.shape
    E, _, F2 = w_gu.shape
    Fe = F2 // 2
    tm = MOE_TILE
    nb = NP // tm
    grid_spec = pltpu.PrefetchScalarGridSpec(
        num_scalar_prefetch=2,
        grid=(nb,),
        in_specs=[
            pl.BlockSpec((tm, D), lambda i, be, nu: (i, 0)),
            pl.BlockSpec((1, D, F2), lambda i, be, nu: (be[i], 0, 0)),
            pl.BlockSpec((1, 1, F2), lambda i, be, nu: (be[i], 0, 0)),
            pl.BlockSpec((1, Fe, D), lambda i, be, nu: (be[i], 0, 0)),
            pl.BlockSpec((1, 1, D), lambda i, be, nu: (be[i], 0, 0)),
        ],
        out_specs=pl.BlockSpec((tm, D), lambda i, be, nu: (i, 0)),
        scratch_shapes=[pltpu.VMEM((D, F2), BF16), pltpu.VMEM((Fe, D), BF16)],
    )
    return pl.pallas_call(
        functools.partial(_moe_kernel, d_expert=Fe),
        out_shape=jax.ShapeDtypeStruct((NP, D), F32),
        grid_spec=grid_spec,
        compiler_params=_cparams(("arbitrary",)),
        name="moe_experts",
    )(block_e, n_used, xs, w_gu, b_gu.reshape(E, 1, F2), w_dn, b_dn.reshape(E, 1, D))


def _final_norm_kernel(h_ref, y_ref, g_ref, w_ref, o_ref):
    h = h_ref[0] + g_ref[0, 0] * y_ref[0]
    o_ref[0] = h * lax.rsqrt(jnp.mean(h * h, axis=-1, keepdims=True) + NORM_EPS) * w_ref[...]


def _final_norm(h, y, gate, w):
    B, L, D = h.shape
    tm = TOKEN_TILE
    tok = pl.BlockSpec((1, tm, D), lambda b, i: (b, i, 0))
    return pl.pallas_call(
        _final_norm_kernel,
        out_shape=jax.ShapeDtypeStruct((B, L, D), F32),
        grid=(B, L // tm),
        in_specs=[tok, tok, pl.BlockSpec((1, 1, 1, D), lambda b, i: (b, 0, 0, 0)),
                  pl.BlockSpec((1, D), lambda b, i: (0, 0))],
        out_specs=tok,
        compiler_params=_cparams(("parallel", "parallel")),
        name="final_norm",
    )(h, y, gate, w)


_HI = lax.Precision.HIGHEST


def _mm(a, b):
    return jnp.einsum('...ij,...jk->...ik', a, b, precision=_HI)


def _tr(x):
    return jnp.swapaxes(x, -1, -2)


def _tri_masks(rev):
    i = np.arange(CHUNK)[:, None]
    j = np.arange(CHUNK)[None, :]
    return ((j > i), (j >= i)) if rev else ((j < i), (j <= i))


def _tri_inv(a):
    i = np.arange(CHUNK)[:, None]
    j = np.arange(CHUNK)[None, :]
    same16 = (i // 16) == (j // 16)
    same32 = (i // 32) == (j // 32)
    d = jnp.where(same16, a, 0.0)
    e = jnp.where(same32 & ~same16, a, 0.0)
    f = jnp.where(~same32, a, 0.0)
    d2 = _mm(d, d)
    d4 = _mm(d2, d2)
    d8 = _mm(d4, d4)
    t = jnp.eye(CHUNK, dtype=a.dtype) + d
    t = t + _mm(t, d2)
    t = t + _mm(t, d4)
    t = t + _mm(t, d8)
    t = t + _mm(_mm(t, e), t)
    t = t + _mm(_mm(t, f), t)
    return t


def _cum(x, rev, axis):
    if rev:
        return jnp.flip(jnp.cumsum(jnp.flip(x, axis), axis), axis)
    return jnp.cumsum(x, axis)


def _rwkv_chunk_prep(r, lw, k, v, a, b, rev):
    strict, incl = _tri_masks(rev)
    c = _cum(lw, rev, -2)
    c_end = c[..., :1, :] if rev else c[..., -1:, :]
    rt = r * jnp.exp(c)
    at = a * jnp.exp(c - lw)
    bt = b * jnp.exp(-c)
    kt = k * jnp.exp(-c)
    bb = b * jnp.exp(c_end - c)
    kb = k * jnp.exp(c_end - c)
    a_ab = jnp.where(strict, _mm(at, _tr(bt)), 0.0)
    a_ak = jnp.where(strict, _mm(at, _tr(kt)), 0.0)
    a_rb = jnp.where(incl, _mm(rt, _tr(bt)), 0.0)
    a_rk = jnp.where(incl, _mm(rt, _tr(kt)), 0.0)
    tm = _tri_inv(a_ab)
    ah = _mm(tm, at)
    uh = _mm(tm, _mm(a_ak, v))
    rh = rt + _mm(a_rb, ah)
    oloc = _mm(a_rk, v) + _mm(a_rb, uh)
    eye = jnp.eye(r.shape[-1], dtype=r.dtype)
    mt = eye * jnp.exp(c_end) + _mm(_tr(bb), ah)
    nt = _mm(_tr(bb), uh) + _mm(_tr(kb), v)
    return rh, oloc, mt, nt


def _gdn_chunk_prep(q, k, v, g, beta, rev):
    strict, incl = _tri_masks(rev)
    kdim = q.shape[-1]
    q = q * (kdim ** -0.5)
    gc = _cum(g, rev, -1)
    gl = gc[..., :1] if rev else gc[..., -1:]
    diff = gc[..., :, None] - gc[..., None, :]
    decay = jnp.where(incl, jnp.exp(jnp.where(incl, diff, 0.0)), 0.0)
    kb = k * beta[..., None]
    m = jnp.where(strict, _mm(kb, _tr(k)) * decay, 0.0)
    tm = _tri_inv(-m)
    u = _mm(tm, v * beta[..., None])
    w = _mm(tm, kb * jnp.exp(gc)[..., None])
    attn = _mm(q, _tr(k)) * decay
    kh = k * jnp.exp(gl - gc)[..., None]
    eye = jnp.eye(kdim, dtype=q.dtype)
    mt = eye * jnp.exp(gl)[..., None] - _mm(_tr(kh), w)
    nt = _mm(_tr(kh), u)
    rh = q * jnp.exp(gc)[..., None] - _mm(attn, w)
    oloc = _mm(attn, u)
    return rh, oloc, mt, nt


def _chunks(t, heads):
    B, T, _ = t.shape
    return jnp.moveaxis(t.reshape(B, T // CHUNK, CHUNK, heads, -1), 3, 1)


def _unchunks(t):
    B, H, n, C, N = t.shape
    return jnp.moveaxis(t, 1, 3).reshape(B, n * C, H * N)


def _scan_both_directions(prep_fn, n_ctx_chunks, chains):
    outs = 0.0
    for d in range(2):
        parts = prep_fn(d)
        n = parts[0].shape[2]
        if d:
            order = np.concatenate([np.arange(n_ctx_chunks)[::-1], np.arange(n_ctx_chunks, n)[::-1]])
        else:
            order = np.arange(n)
        B, H = parts[0].shape[:2]
        flat = [p[:, :, order].reshape((B * H, n) + p.shape[3:]) for p in parts]
        o = _affine_scan(*flat, chains=chains)
        o = o.reshape((B, H, n) + o.shape[2:])[:, :, np.argsort(order)]
        outs = outs + o
    return outs


def _dwconv(u, w):
    K = w.shape[0]
    L = u.shape[1]
    up = jnp.pad(u, ((0, 0), (K // 2, K // 2), (0, 0)))
    return sum(up[:, j:j + L] * w[j] for j in range(K))


def _l2n(t):
    return t * lax.rsqrt(jnp.sum(t * t, axis=-1, keepdims=True) + 1e-12)


def _shift_grid(p):
    B, L, Cc = p.shape
    g = p.reshape(B, L // GRID_W, GRID_W, Cc // 4, 4)
    gp = jnp.pad(g, ((0, 0), (1, 1), (1, 1), (0, 0), (0, 0)))
    return jnp.stack([gp[:, 1:-1, :-2, :, 0], gp[:, 1:-1, 2:, :, 1], gp[:, :-2, 1:-1, :, 2],
                      gp[:, 2:, 1:-1, :, 3]], axis=-1).reshape(B, L, Cc)


def _shift_seq(p):
    B, L, Cc = p.shape
    s = jnp.pad(p.reshape(B, L, Cc // 2, 2), ((0, 0), (1, 1), (0, 0), (0, 0)))
    return jnp.stack([s[:, :-2, :, 0], s[:, 2:, :, 1]], axis=-1).reshape(B, L, Cc)


def _dft_mats(L):
    n2 = 2 * L
    f = lax.broadcasted_iota(jnp.int32, (L, L), 0)
    t = lax.broadcasted_iota(jnp.int32, (L, L), 1)
    ang = ((f * t) % n2).astype(F32) * (2.0 * math.pi / n2)
    return jnp.cos(ang), jnp.sin(ang)


def _hyena_filter(L, w1, b1, w2, b2, w3, b3, w4, freq, d_hy):
    pos = jnp.arange(L, dtype=F32)
    t = jnp.linspace(0.0, 1.0, L, dtype=F32)[:, None]
    bands = jnp.linspace(1e-4, HYENA_BANDS - 1, HYENA_BANDS, dtype=F32)
    ang = (2.0 * math.pi / L) * pos[:, None] * bands
    z = jnp.concatenate([t, jnp.cos(ang), -jnp.sin(ang)], axis=-1)
    hdn = jnp.sin(freq * (jnp.dot(z, w1, precision=_HI) + b1))
    hdn = jnp.sin(freq * (jnp.dot(hdn, w2, precision=_HI) + b2))
    hdn = jnp.sin(freq * (jnp.dot(hdn, w3, precision=_HI) + b3))
    h = jnp.dot(hdn, w4, precision=_HI)
    deltas = jnp.linspace(math.log(HYENA_DECAY_TARGET) / HYENA_SLOW_PCT,
                          math.log(HYENA_DECAY_TARGET) / HYENA_FAST_PCT, d_hy, dtype=F32)
    window = jnp.exp(-t * jnp.abs(deltas))
    h_fwd = h[:, :d_hy] * window
    h_bwd = h[:, d_hy:] * window
    k_lo = h_fwd
    k_hi = jnp.concatenate([jnp.zeros((1, d_hy), F32), h_bwd[:0:-1]], axis=0)
    return k_lo, k_hi


def _hyena_mixer(u, short_w, short_b, fparams, skip, d_hy):
    B, L, _ = u.shape
    uc = _dwconv(u, short_w) + short_b
    x0, x1, v = uc[..., :d_hy], uc[..., d_hy:2 * d_hy], uc[..., 2 * d_hy:]
    v = v * x1
    k_lo, k_hi = _hyena_filter(L, *fparams, d_hy)
    cm, sm = _dft_mats(L)
    sgn = jnp.where(jnp.arange(L) % 2 == 0, 1.0, -1.0).astype(F32)[:, None]
    wf = jnp.where(jnp.arange(L) == 0, 1.0, 2.0).astype(F32)[:, None] / (2 * L)
    p = (jnp.dot(cm, k_lo, precision=_HI) + sgn * jnp.dot(cm, k_hi, precision=_HI)) * wf
    q = (jnp.dot(sm, k_lo, precision=_HI) + sgn * jnp.dot(sm, k_hi, precision=_HI)) * wf
    pn = jnp.sum(sgn * (k_lo + k_hi), axis=0, keepdims=True) / (2 * L)
    y = _hyena_conv(v, cm.astype(BF16), sm.astype(BF16), p, q, pn, sgn)
    return (y + v * skip) * x0


def _rwkv_prepare(m, w0, w2, a0, a2, g2, k_k, k_a, H):
    B, T, _ = m.shape
    N = RWKV_HEAD_DIM
    dr = H * N
    lr_w = w2.shape[1]
    lr_a = a2.shape[1]
    o3 = 3 * dr
    o4 = o3 + 2 * lr_w
    o5 = o4 + 2 * lr_a
    r, k, v = m[..., :dr], m[..., dr:2 * dr], m[..., 2 * dr:o3]
    wd = m[..., o3:o4].reshape(B, T, 2, lr_w)
    ad = m[..., o4:o5].reshape(B, T, 2, lr_a)
    gd = m[..., o5:]
    lw = -jax.nn.softplus(-(w0 + jnp.einsum('bldr,drc->bldc', jnp.tanh(wd), w2, precision=_HI))) - 0.5
    logw = -jnp.exp(lw)
    a = jax.nn.sigmoid(a0 + jnp.einsum('bldr,drc->bldc', ad, a2, precision=_HI))
    g = jnp.dot(jax.nn.sigmoid(gd), g2, precision=_HI)
    kk = _l2n((k * k_k).reshape(B, T, H, N)).reshape(B, T, dr)
    k_dir = k[:, :, None] * (1 + (a - 1) * k_a)
    return r, k_dir, v, kk, logw, a, g


def _rwkv_mixer(p, mu, w0, w2, a0, a2, g2, k_k, k_a, r_k, gn_w, gn_b, n_ctx):
    B, T, _ = p.shape
    H = w0.shape[-1] // RWKV_HEAD_DIM
    N = RWKV_HEAD_DIM
    sh = jnp.concatenate([_shift_seq(p[:, :n_ctx]), _shift_grid(p[:, n_ctx:])], axis=1)
    m = p + mu * (sh - p)
    r, k_dir, v, kk, logw, a, g = _rwkv_prepare(m, w0, w2, a0, a2, g2, k_k, k_a, H)

    def prep(d):
        seq = (r, logw[:, :, d], k_dir[:, :, d], v, -kk, kk * a[:, :, d])
        return _rwkv_chunk_prep(*[_chunks(t, H) for t in seq], rev=bool(d))

    o = _unchunks(_scan_both_directions(prep, n_ctx // CHUNK, chains=8)).reshape(B, T, H, N)
    mean = jnp.mean(o, axis=-1, keepdims=True)
    var = jnp.mean(jnp.square(o - mean), axis=-1, keepdims=True)
    y = ((o - mean) * lax.rsqrt(var + RWKV_GN_EPS)).reshape(B, T, H * N) * gn_w + gn_b
    bonus = jnp.sum((r * (k_dir[:, :, 0] + k_dir[:, :, 1]) * r_k).reshape(B, T, H, N), axis=-1, keepdims=True)
    y = y + (bonus * v.reshape(B, T, H, N)).reshape(B, T, H * N)
    return y * g


def _gdn_mixer(p, conv_w, a_log, dt_bias, norm_w, n_ctx):
    B, T, _ = p.shape
    K = GDN_HEAD_DIM
    H = a_log.shape[-1]
    dg = H * K
    x = p[..., :3 * dg]
    qkv = jax.nn.silu(jnp.concatenate([_dwconv(x[:, :n_ctx], conv_w), _dwconv(x[:, n_ctx:], conv_w)], axis=1))
    q = _l2n(qkv[..., :dg].reshape(B, T, H, K)).reshape(B, T, dg)
    k = _l2n(qkv[..., dg:2 * dg].reshape(B, T, H, K)).reshape(B, T, dg)
    v = qkv[..., 2 * dg:]
    z = p[..., 3 * dg:4 * dg]
    rest = p[..., 4 * dg:]
    beta = jax.nn.sigmoid(rest[..., :2 * H].reshape(B, T, 2, H))
    g = -jnp.exp(a_log) * jax.nn.softplus(rest[..., 2 * H:].reshape(B, T, 2, H) + dt_bias)

    def prep(d):
        gd = jnp.moveaxis(g[:, :, d], 2, 1).reshape(B, H, T // CHUNK, CHUNK)
        bd = jnp.moveaxis(beta[:, :, d], 2, 1).reshape(B, H, T // CHUNK, CHUNK)
        return _gdn_chunk_prep(_chunks(q, H), _chunks(k, H), _chunks(v, H), gd, bd, rev=bool(d))

    o = _unchunks(_scan_both_directions(prep, n_ctx // CHUNK, chains=4)).reshape(B, T, H, K)
    o = o * lax.rsqrt(jnp.mean(o * o, axis=-1, keepdims=True) + NORM_EPS) * norm_w
    return (o * jax.nn.silu(z).reshape(B, T, H, K)).reshape(B, T, dg)


def _moe(n_bf, logits, w_gu, b_gu, w_dn, b_dn):
    Nt, D = n_bf.shape
    E = w_gu.shape[0]
    tm = MOE_TILE
    top_logit, top_idx = lax.top_k(logits, TOP_K)
    gates = jax.nn.softmax(top_logit, axis=-1)
    A = Nt * TOP_K
    flat_e = top_idx.reshape(A)
    onehot = (flat_e[:, None] == jnp.arange(E)[None, :]).astype(jnp.int32)
    rank = jnp.cumsum(onehot, axis=0) - onehot
    counts = jnp.sum(onehot, axis=0)
    blocks_per = (counts + tm - 1) // tm
    block_end = jnp.cumsum(blocks_per)
    padded_starts = (block_end - blocks_per) * tm
    dest = padded_starts[flat_e] + jnp.sum(rank * onehot, axis=1)
    n_blocks = -(-A // tm) + E
    slot_tok = jnp.full((n_blocks * tm,), Nt, jnp.int32).at[dest].set(jnp.arange(A, dtype=jnp.int32) // TOP_K)
    block_e = jnp.minimum(jnp.searchsorted(block_end, jnp.arange(n_blocks), side='right'), E - 1).astype(jnp.int32)
    n_used = block_end[-1:].astype(jnp.int32)
    x_pad = jnp.concatenate([n_bf, jnp.zeros((1, D), n_bf.dtype)], axis=0)
    xs = x_pad[slot_tok]
    yb = _moe_experts(xs, block_e, n_used, w_gu, b_gu, w_dn, b_dn)
    yk = yb[dest].reshape(Nt, TOP_K, D)
    return jnp.sum(yk * gates[..., None], axis=1)


def kernel(x, c, ctx, c_ctx, ada_w, ada_b, norm_mix_w, norm_ffn_w, final_norm_w, w_in, w_out, hy_short_w, hy_short_b, hy_f_w1, hy_f_b1, hy_f_w2, hy_f_b2, hy_f_w3, hy_f_b3, hy_f_w4, hy_f_freq, hy_skip, rw_mu, rw_w0, rw_w2, rw_a0, rw_a2, rw_g2, rw_k_k, rw_k_a, rw_r_k, rw_gn_w, rw_gn_b, gdn_conv_w, gdn_a_log, gdn_dt_bias, gdn_norm_w, moe_router_w, moe_router_b, moe_w_gu, moe_b_gu, moe_w_dn, moe_b_dn):
    B, L, D = x.shape
    Lc = ctx.shape[1]
    depth = ada_w.shape[0]
    d_hy = hy_skip.shape[-1]
    hy_cols = 3 * d_hy
    rw_cols = rw_mu.shape[-1]
    gd_cols = w_in.shape[-1] - hy_cols - rw_cols
    splits = (hy_cols, rw_cols, gd_cols)
    nct = Lc // TOKEN_TILE

    h = jnp.concatenate([ctx, x], axis=1)
    s_lat = jax.nn.silu(c)
    s_ctx = jax.nn.silu(c_ctx)
    y_moe = None
    mods = None
    for l in range(depth):
        last = l == depth - 1
        ml = jnp.dot(s_lat, ada_w[l], precision=_HI) + ada_b[l]
        mc = jnp.dot(s_ctx, ada_w[l], precision=_HI) + ada_b[l]
        mods = jnp.stack([jnp.broadcast_to(mc, ml.shape), ml], axis=1).reshape(B, 2, N_MOD, 1, D)
        mod = lambda j: mods[:, :, j]
        p_hy, p_rw, p_gd = _in_proj(h, mod(0), mod(1), norm_mix_w[l][None], w_in[l].astype(BF16), splits, nct)

        fparams = (hy_f_w1[l], hy_f_b1[l], hy_f_w2[l], hy_f_b2[l], hy_f_w3[l], hy_f_b3[l], hy_f_w4[l], hy_f_freq[l])
        y_hy_l = _hyena_mixer(p_hy[:, Lc:], hy_short_w[l], hy_short_b[l], fparams, hy_skip[l], d_hy)
        if last:
            y_hy_c = jnp.zeros((B, Lc, d_hy), F32)
        else:
            y_hy_c = _hyena_mixer(p_hy[:, :Lc], hy_short_w[l], hy_short_b[l], fparams, hy_skip[l], d_hy)
        y_hy = jnp.concatenate([y_hy_c, y_hy_l], axis=1)
        y_rw = _rwkv_mixer(p_rw, rw_mu[l], rw_w0[l], rw_w2[l], rw_a0[l], rw_a2[l], rw_g2[l], rw_k_k[l],
                           rw_k_a[l], rw_r_k[l], rw_gn_w[l], rw_gn_b[l], Lc)
        y_gd = _gdn_mixer(p_gd, gdn_conv_w[l], gdn_a_log[l], gdn_dt_bias[l], gdn_norm_w[l], Lc)

        h, n_bf, logits = _out_proj(h, y_hy, y_rw, y_gd, w_out[l].astype(BF16), mod(2), mod(3), mod(4),
                                    norm_ffn_w[l][None], moe_router_w[l], moe_router_b[l][None], nct)
        if last:
            n_tok = n_bf[:, Lc:].reshape(B * L, D)
            lg = logits[:, Lc:].reshape(B * L, -1)
        else:
            n_tok = n_bf.reshape(B * (Lc + L), D)
            lg = logits.reshape(B * (Lc + L), -1)
        y_moe = _moe(n_tok, lg, moe_w_gu[l], moe_b_gu[l], moe_w_dn[l], moe_b_dn[l])
        if not last:
            gate = jnp.concatenate([jnp.broadcast_to(mods[:, 0, 5], (B, Lc, D)),
                                    jnp.broadcast_to(mods[:, 1, 5], (B, L, D))], axis=1)
            h = h + gate * y_moe.reshape(B, Lc + L, D)
    return _final_norm(h[:, Lc:], y_moe.reshape(B, L, D), mods[:, 1:2, 5], final_norm_w[None])
```

```python
import functools
import math

import jax
import jax.numpy as jnp
import numpy as np
from jax import lax
from jax.experimental import pallas as pl
from jax.experimental.pallas import tpu as pltpu

F32 = jnp.float32
BF16 = jnp.bfloat16

NORM_EPS = 1e-6
N_MOD = 6
GRID_W = 64
CHUNK = 64
HEADS = 4
HYENA_BANDS = 16
HYENA_DECAY_TARGET = 1e-2
HYENA_FAST_PCT = 0.3
HYENA_SLOW_PCT = 1.5
RWKV_HEAD_DIM = 64
RWKV_GN_EPS = 64e-5
GDN_HEAD_DIM = 128
GDN_CONV_HALO = 8
TOP_K = 4
SWIGLU_ALPHA = 1.702
SWIGLU_LIMIT = 7.0

TOKEN_TILE = 256
MOE_TILE = 256
VMEM_LIMIT = 56 * 1024 * 1024
_HI = lax.Precision.HIGHEST


def _cparams(sem):
    return pltpu.CompilerParams(dimension_semantics=sem, vmem_limit_bytes=VMEM_LIMIT)


def _dot(a, b):
    return jnp.dot(a, b, preferred_element_type=F32)


def _split2(x):
    hi = x.astype(BF16)
    lo = (x - hi.astype(F32)).astype(BF16)
    return hi, lo


def _split3(x):
    hi = x.astype(BF16)
    r1 = x - hi.astype(F32)
    mid = r1.astype(BF16)
    lo = (r1 - mid.astype(F32)).astype(BF16)
    return hi, mid, lo


def _dot3(a, b):
    ah, al = _split2(a)
    bh, bl = _split2(b)
    return _dot(ah, bh) + _dot(al, bh) + _dot(ah, bl)


def _dot_xl(sel_bf, x):
    hi, mid, lo = _split3(x)
    return _dot(sel_bf, hi) + _dot(sel_bf, mid) + _dot(sel_bf, lo)


def _dot_xr(x, sel_bf):
    hi, mid, lo = _split3(x)
    return _dot(hi, sel_bf) + _dot(mid, sel_bf) + _dot(lo, sel_bf)


def _f(mask):
    return jnp.where(mask, 1.0, 0.0).astype(F32)


def _block_diag_mask(rows, cols, rblk, cblk):
    r = lax.broadcasted_iota(jnp.int32, (rows, cols), 0) >> int(math.log2(rblk))
    c = lax.broadcasted_iota(jnp.int32, (rows, cols), 1) >> int(math.log2(cblk))
    return r == c


def _softplus(x):
    return jnp.maximum(x, 0.0) + jnp.log1p(jnp.exp(-jnp.abs(x)))


def _chunk_masks(rev):
    shape = (CHUNK, HEADS * CHUNK)
    i = lax.broadcasted_iota(jnp.int32, shape, 0)
    j = lax.broadcasted_iota(jnp.int32, shape, 1) & (CHUNK - 1)
    strict = (j > i) if rev else (j < i)
    incl = (j >= i) if rev else (j <= i)
    same16 = (i >> 4) == (j >> 4)
    same32 = (i >> 5) == (j >> 5)
    ii = lax.broadcasted_iota(jnp.int32, (CHUNK, CHUNK), 0)
    jj = lax.broadcasted_iota(jnp.int32, (CHUNK, CHUNK), 1)
    tri = _f((jj >= ii) if rev else (jj <= ii)).astype(BF16)
    return dict(strict=_f(strict), incl=_f(incl), incl_b=incl, eye=_f(i == j), m16=_f(same16),
                m32=_f(jnp.logical_and(same32, jnp.logical_not(same16))), m64=_f(jnp.logical_not(same32)),
                tri=tri, bd=_f(_block_diag_mask(HEADS * CHUNK, HEADS * CHUNK, CHUNK, CHUNK)))


def _mmh(x, y, bd):
    return _dot3(x, jnp.concatenate([y] * HEADS, axis=0) * bd)


def _tri_inv(a, mk):
    bd = mk['bd']
    d = a * mk['m16']
    e = a * mk['m32']
    f = a * mk['m64']
    d2 = _mmh(d, d, bd)
    d4 = _mmh(d2, d2, bd)
    d8 = _mmh(d4, d4, bd)
    t = mk['eye'] + d
    t = t + _mmh(t, d2, bd)
    t = t + _mmh(t, d4, bd)
    t = t + _mmh(t, d8, bd)
    t = t + _mmh(_mmh(t, e, bd), t, bd)
    t = t + _mmh(_mmh(t, f, bd), t, bd)
    return t


def _tile_of_step(s, nct, n_tiles, rev):
    if not rev:
        return s
    return jnp.where(s < nct, nct - 1 - s, n_tiles - 1 - (s - nct))


def _in_proj_kernel(h_ref, shift_ref, scale_ref, nw_ref, w_ref, hy_ref, rw_ref, gd_ref, *, splits):
    h = h_ref[0]
    n = h * lax.rsqrt(jnp.mean(h * h, axis=-1, keepdims=True) + NORM_EPS) * nw_ref[...]
    n = (n * (1.0 + scale_ref[0, 0]) + shift_ref[0, 0]).astype(BF16)
    c0, c1, c2 = splits
    hy_ref[0] = _dot(n, w_ref[:, :c0])
    rw_ref[0] = _dot(n, w_ref[:, c0:c0 + c1])
    gd_ref[0] = _dot(n, w_ref[:, c0 + c1:c0 + c1 + c2])


def _in_proj(h, shift, scale, norm_w, w_bf, splits, n_ctx_tiles):
    B, T, D = h.shape
    tm = TOKEN_TILE
    seg = lambda b, i: (b, (i >= n_ctx_tiles).astype(jnp.int32), 0, 0)
    cols = w_bf.shape[1]
    outs = [jax.ShapeDtypeStruct((B, T, c), F32) for c in splits]
    return pl.pallas_call(
        functools.partial(_in_proj_kernel, splits=splits),
        out_shape=outs,
        grid=(B, T // tm),
        in_specs=[
            pl.BlockSpec((1, tm, D), lambda b, i: (b, i, 0)),
            pl.BlockSpec((1, 1, 1, D), seg),
            pl.BlockSpec((1, 1, 1, D), seg),
            pl.BlockSpec((1, D), lambda b, i: (0, 0)),
            pl.BlockSpec((D, cols), lambda b, i: (0, 0)),
        ],
        out_specs=[pl.BlockSpec((1, tm, c), lambda b, i: (b, i, 0)) for c in splits],
        compiler_params=_cparams(("parallel", "parallel")),
        name="in_proj",
    )(h, shift, scale, norm_w, w_bf)


def _rwkv_chunk(r, lw, k, v, a, b, s_ref, mk):
    bd = mk['bd']
    c = _dot_xl(mk['tri'], lw)
    c_end = c[0:1] if mk['rev'] else c[CHUNK - 1:CHUNK]
    ec = jnp.exp(c)
    enc = jnp.exp(-c)
    e2 = jnp.exp(c_end - c)
    rt = r * ec
    at = a * jnp.exp(c - lw)
    bt = b * enc
    kt = k * enc
    bb = b * e2
    kb = k * e2
    btT = jnp.concatenate([bt] * HEADS, axis=0).T * bd
    ktT = jnp.concatenate([kt] * HEADS, axis=0).T * bd
    res = _dot3(jnp.concatenate([at, rt], axis=0), jnp.concatenate([btT, ktT], axis=1))
    hc = HEADS * CHUNK
    a_ab = res[:CHUNK, :hc] * mk['strict']
    a_ak = res[:CHUNK, hc:] * mk['strict']
    a_rb = res[CHUNK:, :hc] * mk['incl']
    a_rk = res[CHUNK:, hc:] * mk['incl']
    t = _tri_inv(a_ab, mk)
    akv = _mmh(a_ak, v, bd)
    ah = _mmh(t, at, bd)
    uh = _mmh(t, akv, bd)
    rh = rt + _mmh(a_rb, ah, bd)
    oloc = _mmh(a_rk, v, bd) + _mmh(a_rb, uh, bd)
    lhs = jnp.concatenate([bb, kb], axis=0).T
    rhs = jnp.concatenate([jnp.concatenate([ah, uh], axis=1),
                           jnp.concatenate([jnp.zeros_like(v), v], axis=1)], axis=0)
    mn = _dot3(lhs, rhs)
    hk = a.shape[1]
    mt = mn[:, :hk] * bd + mk['eye_k'] * jnp.exp(c_end)
    nt = mn[:, hk:] * bd
    s = s_ref[...]
    out = _dot3(rh, s) + oloc
    s_ref[...] = _dot3(mt, s) + nt
    return out


def _rwkv_dir_kernel(prev_ref, cur_ref, next_ref, mu_ref, w2_ref, w0_ref, a2_ref, a0_ref, g2_ref, kk_ref,
                     ka_ref, rk_ref, o_ref, bv_ref, g_ref, buf, s_ref, *, rev, nct, n_tiles, dr):
    step = pl.program_id(1)
    t = _tile_of_step(step, nct, n_tiles, rev)
    tm = TOKEN_TILE
    hw = GRID_W

    @pl.when(step == 0)
    def _():
        s_ref[...] = jnp.zeros_like(s_ref)

    is_ctx = t < nct
    first = jnp.logical_or(t == 0, t == nct)
    last = jnp.logical_or(t == nct - 1, t == n_tiles - 1)
    cur = cur_ref[0]
    buf[0:hw] = jnp.where(first, 0.0, prev_ref[0])
    buf[hw:hw + tm] = cur
    buf[hw + tm:hw + tm + hw] = jnp.where(last, 0.0, next_ref[0])
    cols = cur.shape[1]
    left = buf[hw - 1:hw - 1 + tm]
    right = buf[hw + 1:hw + 1 + tm]
    up = buf[0:tm]
    down = buf[2 * hw:2 * hw + tm]
    col = lax.broadcasted_iota(jnp.int32, (tm, 1), 0) & (hw - 1)
    lane = lax.broadcasted_iota(jnp.int32, (1, cols), 1)
    left_g = jnp.where(col == 0, 0.0, left)
    right_g = jnp.where(col == hw - 1, 0.0, right)
    l4 = lane & 3
    sh_lat = jnp.where(l4 == 0, left_g, jnp.where(l4 == 1, right_g, jnp.where(l4 == 2, up, down)))
    sh_ctx = jnp.where((lane & 1) == 0, left, right)
    sh = jnp.where(is_ctx, sh_ctx, sh_lat)
    m = cur + mu_ref[...] * (sh - cur)

    r = m[:, :dr]
    k = m[:, dr:2 * dr]
    v = m[:, 2 * dr:3 * dr]
    lw_w = w2_ref.shape[0]
    lw_a = a2_ref.shape[0]
    o3 = 3 * dr
    wd = m[:, o3:o3 + lw_w]
    ad = m[:, o3 + lw_w:o3 + lw_w + lw_a]
    gd = m[:, o3 + lw_w + lw_a:]
    lw = -_softplus(-(w0_ref[...] + _dot3(jnp.tanh(wd), w2_ref[...]))) - 0.5
    logw = -jnp.exp(lw)
    a_lr = jax.nn.sigmoid(a0_ref[...] + _dot3(ad, a2_ref[...]))
    ones_bd = _f(_block_diag_mask(dr, dr, RWKV_HEAD_DIM, RWKV_HEAD_DIM)).astype(BF16)
    kx = k * kk_ref[...]
    kk = kx * lax.rsqrt(_dot_xr(kx * kx, ones_bd) + 1e-12)
    k_d = k * (1.0 + (a_lr - 1.0) * ka_ref[...])
    b_d = kk * a_lr
    a_s = -kk
    bv_ref[0] = _dot_xr(r * k_d * rk_ref[...], ones_bd) * v
    g_ref[0] = _dot3(jax.nn.sigmoid(gd), g2_ref[...])

    mk = _chunk_masks(rev)
    mk['rev'] = rev
    mk['eye_k'] = _f(lax.broadcasted_iota(jnp.int32, (dr, dr), 0) == lax.broadcasted_iota(jnp.int32, (dr, dr), 1))
    nch = tm // CHUNK
    for ci in (range(nch - 1, -1, -1) if rev else range(nch)):
        sl = slice(ci * CHUNK, (ci + 1) * CHUNK)
        o_ref[0, sl, :] = _rwkv_chunk(r[sl], logw[sl], k_d[sl], v[sl], a_s[sl], b_d[sl], s_ref, mk)


def _rwkv_dir(p, mu, w2p, w0, a2p, a0, g2, k_k, k_a, r_k, nct, rev):
    B, T, cols = p.shape
    tm = TOKEN_TILE
    n_tiles = T // tm
    dr = g2.shape[1]
    hpt = tm // GRID_W
    nhb = T // GRID_W
    tile = lambda s: _tile_of_step(s, nct, n_tiles, rev)
    full = lambda a: pl.BlockSpec(a.shape, lambda b, s: (0,) * a.ndim)
    tok = lambda c: pl.BlockSpec((1, tm, c), lambda b, s: (b, tile(s), 0))
    out_sds = jax.ShapeDtypeStruct((B, T, dr), F32)
    return pl.pallas_call(
        functools.partial(_rwkv_dir_kernel, rev=rev, nct=nct, n_tiles=n_tiles, dr=dr),
        out_shape=[out_sds, out_sds, out_sds],
        grid=(B, n_tiles),
        in_specs=[
            pl.BlockSpec((1, GRID_W, cols), lambda b, s: (b, jnp.maximum(tile(s) * hpt - 1, 0), 0)),
            tok(cols),
            pl.BlockSpec((1, GRID_W, cols), lambda b, s: (b, jnp.minimum((tile(s) + 1) * hpt, nhb - 1), 0)),
            full(mu), full(w2p), full(w0), full(a2p), full(a0), full(g2), full(k_k), full(k_a), full(r_k),
        ],
        out_specs=[tok(dr), tok(dr), tok(dr)],
        scratch_shapes=[pltpu.VMEM((tm + 2 * GRID_W, cols), F32), pltpu.VMEM((dr, dr), F32)],
        compiler_params=_cparams(("parallel", "arbitrary")),
        name="rwkv_rev" if rev else "rwkv_fwd",
    )(p, p, p, mu, w2p, w0, a2p, a0, g2, k_k, k_a, r_k)


def _gdn_chunk(q, k, v, beta_i, beta_k, g_i, g_k, s_ref, mk):
    kd = GDN_HEAD_DIM
    gc_i = _dot_xl(mk['tri'], g_i)
    gc_k = _dot_xl(mk['tri'], g_k)
    end = 0 if mk['rev'] else CHUNK - 1
    gl_i = gc_i[end:end + 1]
    gl_k = gc_k[end:end + 1]
    gc_j = jnp.sum(gc_i * mk['eye'], axis=0, keepdims=True)
    incl = mk['incl_b']
    decay = jnp.where(incl, jnp.exp(jnp.where(incl, gc_i - gc_j, 0.0)), 0.0)
    kb = k * beta_k
    ktT = jnp.concatenate([k] * HEADS, axis=0).T * mk['bd_kt']
    res = _dot3(jnp.concatenate([kb, q], axis=0), ktT)
    m = res[:CHUNK] * decay * mk['strict']
    attn = res[CHUNK:] * decay
    t = _tri_inv(-m, mk)
    x = jnp.concatenate([v * beta_k, kb * jnp.exp(gc_k)], axis=1)
    uw = _mmh(t, x, mk['bd_x'])
    hk = HEADS * kd
    u = uw[:, :hk]
    w = uw[:, hk:]
    au_aw = _mmh(attn, uw, mk['bd_x'])
    rh = q * jnp.exp(gc_k) - au_aw[:, hk:]
    oloc = au_aw[:, :hk]
    kh = k * jnp.exp(gl_k - gc_k)
    khT = jnp.concatenate([kh, jnp.zeros_like(kh)], axis=0).T
    zero = jnp.zeros((CHUNK, 2 * kd), F32)
    outs = []
    for h in range(HEADS):
        sl = slice(h * kd, (h + 1) * kd)
        wu = jnp.concatenate([jnp.concatenate([w[:, sl], u[:, sl]], axis=1), zero], axis=0)
        mn = _dot3(khT[sl], wu)
        mt = mk['eye_k'] * jnp.exp(gl_k[:, sl]) - mn[:, :kd]
        nt = mn[:, kd:]
        s = s_ref[h]
        outs.append(_dot3(rh[:, sl], s) + oloc[:, sl])
        s_ref[h] = _dot3(mt, s) + nt
    return jnp.concatenate(outs, axis=1)


def _gdn_dir_kernel(prev_ref, cur_ref, next_ref, cw_ref, selb_i, selb_k, selg_i, selg_k, na_i, na_k, dt_i, dt_k,
                    o_ref, buf, s_ref, *, rev, nct, n_tiles, dg):
    step = pl.program_id(1)
    t = _tile_of_step(step, nct, n_tiles, rev)
    tm = TOKEN_TILE
    hl = GDN_CONV_HALO

    @pl.when(step == 0)
    def _():
        s_ref[...] = jnp.zeros_like(s_ref)

    first = jnp.logical_or(t == 0, t == nct)
    last = jnp.logical_or(t == nct - 1, t == n_tiles - 1)
    c3 = 3 * dg
    buf[0:hl] = jnp.where(first, 0.0, prev_ref[0])
    buf[hl:hl + tm] = cur_ref[0, :, :c3]
    buf[hl + tm:hl + tm + hl] = jnp.where(last, 0.0, next_ref[0])
    taps = cw_ref.shape[0]
    acc = None
    for j in range(taps):
        off = hl + j - taps // 2
        term = buf[off:off + tm] * cw_ref[j:j + 1, :]
        acc = term if acc is None else acc + term
    qkv = acc * jax.nn.sigmoid(acc)
    ones_bd = _f(_block_diag_mask(dg, dg, GDN_HEAD_DIM, GDN_HEAD_DIM)).astype(BF16)
    q = qkv[:, :dg]
    k = qkv[:, dg:2 * dg]
    v = qkv[:, 2 * dg:]
    q = q * lax.rsqrt(_dot_xr(q * q, ones_bd) + 1e-12) * (GDN_HEAD_DIM ** -0.5)
    k = k * lax.rsqrt(_dot_xr(k * k, ones_bd) + 1e-12)
    rest = cur_ref[0, :, 4 * dg:]
    beta_i = jax.nn.sigmoid(_dot_xr(rest, selb_i[...]))
    beta_k = jax.nn.sigmoid(_dot_xr(rest, selb_k[...]))
    g_i = na_i[...] * _softplus(_dot_xr(rest, selg_i[...]) + dt_i[...])
    g_k = na_k[...] * _softplus(_dot_xr(rest, selg_k[...]) + dt_k[...])

    mk = _chunk_masks(rev)
    mk['rev'] = rev
    kd = GDN_HEAD_DIM
    mk['eye_k'] = _f(lax.broadcasted_iota(jnp.int32, (kd, kd), 0) == lax.broadcasted_iota(jnp.int32, (kd, kd), 1))
    mk['bd_kt'] = _f(_block_diag_mask(dg, HEADS * CHUNK, kd, CHUNK))
    r_ = lax.broadcasted_iota(jnp.int32, (HEADS * CHUNK, 2 * dg), 0) >> int(math.log2(CHUNK))
    c_ = (lax.broadcasted_iota(jnp.int32, (HEADS * CHUNK, 2 * dg), 1) >> int(math.log2(kd))) & (HEADS - 1)
    mk['bd_x'] = _f(r_ == c_)
    nch = tm // CHUNK
    for ci in (range(nch - 1, -1, -1) if rev else range(nch)):
        sl = slice(ci * CHUNK, (ci + 1) * CHUNK)
        o_ref[0, sl, :] = _gdn_chunk(q[sl], k[sl], v[sl], beta_i[sl], beta_k[sl], g_i[sl], g_k[sl], s_ref, mk)


def _gdn_dir(p, conv_w, consts, nct, rev, dg):
    B, T, cols = p.shape
    tm = TOKEN_TILE
    n_tiles = T // tm
    hl = GDN_CONV_HALO
    hpt = tm // hl
    nhb = T // hl
    c3 = 3 * dg
    tile = lambda s: _tile_of_step(s, nct, n_tiles, rev)
    full = lambda a: pl.BlockSpec(a.shape, lambda b, s: (0,) * a.ndim)
    return pl.pallas_call(
        functools.partial(_gdn_dir_kernel, rev=rev, nct=nct, n_tiles=n_tiles, dg=dg),
        out_shape=jax.ShapeDtypeStruct((B, T, dg), F32),
        grid=(B, n_tiles),
        in_specs=[
            pl.BlockSpec((1, hl, c3), lambda b, s: (b, jnp.maximum(tile(s) * hpt - 1, 0), 0)),
            pl.BlockSpec((1, tm, cols), lambda b, s: (b, tile(s), 0)),
            pl.BlockSpec((1, hl, c3), lambda b, s: (b, jnp.minimum((tile(s) + 1) * hpt, nhb - 1), 0)),
            full(conv_w)] + [full(a) for a in consts],
        out_specs=pl.BlockSpec((1, tm, dg), lambda b, s: (b, tile(s), 0)),
        scratch_shapes=[pltpu.VMEM((tm + 2 * hl, c3), F32), pltpu.VMEM((HEADS, GDN_HEAD_DIM, GDN_HEAD_DIM), F32)],
        compiler_params=_cparams(("parallel", "arbitrary")),
        name="gdn_rev" if rev else "gdn_fwd",
    )(p, p, p, conv_w, *consts)


def _gdn_consts(a_log_d, dt_bias_d, d, dg):
    H = HEADS
    ncol = 4 * H
    def sel(base, width):
        m = np.zeros((ncol, H * width), np.float32)
        for h in range(H):
            m[base + d * H + h, h * width:(h + 1) * width] = 1.0
        return jnp.asarray(m, BF16)
    neg_a = -jnp.exp(a_log_d)
    exp_i = lambda x: jnp.repeat(x, CHUNK)[None, :]
    exp_k = lambda x: jnp.repeat(x, GDN_HEAD_DIM)[None, :]
    return (sel(0, CHUNK), sel(0, GDN_HEAD_DIM), sel(2 * H, CHUNK), sel(2 * H, GDN_HEAD_DIM),
            exp_i(neg_a), exp_k(neg_a), exp_i(dt_bias_d), exp_k(dt_bias_d))


def _out_proj_kernel(h_ref, hy_ref, ro0_ref, ro1_ref, bv0_ref, bv1_ref, rg_ref, go0_ref, go1_ref, z_ref,
                     gnw_ref, gnb_ref, gdw_ref, w_ref, gate_ref, shift_ref, scale_ref, nw_ref, rwt_ref, rb_ref,
                     hn_ref, n_ref, lg_ref, *, splits):
    c0, c1, c2 = splits
    o = ro0_ref[0] + ro1_ref[0]
    ones_r = _f(_block_diag_mask(c1, c1, RWKV_HEAD_DIM, RWKV_HEAD_DIM)).astype(BF16)
    mean = _dot_xr(o, ones_r) * (1.0 / RWKV_HEAD_DIM)
    cen = o - mean
    var = _dot_xr(cen * cen, ones_r) * (1.0 / RWKV_HEAD_DIM)
    y_rw = (cen * lax.rsqrt(var + RWKV_GN_EPS) * gnw_ref[...] + gnb_ref[...] + bv0_ref[0] + bv1_ref[0]) * rg_ref[0]
    og = go0_ref[0] + go1_ref[0]
    ones_g = _f(_block_diag_mask(c2, c2, GDN_HEAD_DIM, GDN_HEAD_DIM)).astype(BF16)
    ms = _dot_xr(og * og, ones_g) * (1.0 / GDN_HEAD_DIM)
    z = z_ref[0]
    y_gd = og * lax.rsqrt(ms + NORM_EPS) * gdw_ref[...] * (z * jax.nn.sigmoid(z))

    acc = _dot(hy_ref[0].astype(BF16), w_ref[:c0, :])
    acc += _dot(y_rw.astype(BF16), w_ref[c0:c0 + c1, :])
    acc += _dot(y_gd.astype(BF16), w_ref[c0 + c1:c0 + c1 + c2, :])
    h = h_ref[0] + gate_ref[0, 0] * acc
    hn_ref[0] = h
    n = h * lax.rsqrt(jnp.mean(h * h, axis=-1, keepdims=True) + NORM_EPS) * nw_ref[...]
    n = n * (1.0 + scale_ref[0, 0]) + shift_ref[0, 0]
    n_ref[0] = n.astype(BF16)
    lg_ref[0] = _dot3(n, rwt_ref[...]) + rb_ref[...]


def _out_proj(h, y_hy, rw_parts, gd_parts, p_gd, gn_w, gn_b, gd_w, w_bf, gate, shift, scale, norm_w,
              router_w, router_b, n_ctx_tiles):
    B, T, D = h.shape
    tm = TOKEN_TILE
    E = router_w.shape[1]
    splits = (y_hy.shape[-1], rw_parts[0].shape[-1], gd_parts[0].shape[-1])
    c2 = splits[2]
    seg = lambda b, i: (b, (i >= n_ctx_tiles).astype(jnp.int32), 0, 0)
    tok = lambda c: pl.BlockSpec((1, tm, c), lambda b, i: (b, i, 0))
    full = lambda s: pl.BlockSpec(s, lambda b, i: (0,) * len(s))
    z_spec = pl.BlockSpec((1, tm, c2), lambda b, i: (b, i, 3))
    return pl.pallas_call(
        functools.partial(_out_proj_kernel, splits=splits),
        out_shape=[jax.ShapeDtypeStruct((B, T, D), F32), jax.ShapeDtypeStruct((B, T, D), BF16),
                   jax.ShapeDtypeStruct((B, T, E), F32)],
        grid=(B, T // tm),
        in_specs=[tok(D), tok(splits[0])] + [tok(splits[1])] * 5 + [tok(c2), tok(c2), z_spec,
                  full((1, splits[1])), full((1, splits[1])), full((1, c2)), full(w_bf.shape),
                  pl.BlockSpec((1, 1, 1, D), seg), pl.BlockSpec((1, 1, 1, D), seg),
                  pl.BlockSpec((1, 1, 1, D), seg), full((1, D)), full(router_w.shape), full((1, E))],
        out_specs=[tok(D), tok(D), tok(E)],
        compiler_params=_cparams(("parallel", "parallel")),
        name="out_proj",
    )(h, y_hy, *rw_parts, *gd_parts, p_gd, gn_w, gn_b, gd_w, w_bf, gate, shift, scale, norm_w, router_w, router_b)


def _hyena_conv_kernel(v_ref, cm_ref, sm_ref, p_ref, q_ref, pn_ref, sgn_ref, y_ref):
    v = v_ref[0]
    vb = v.astype(BF16)
    a = _dot(cm_ref[...], vb)
    b = _dot(sm_ref[...], vb)
    p = p_ref[...]
    q = q_ref[...]
    yr = (a * p - b * q).astype(BF16)
    yi = (a * q + b * p).astype(BF16)
    y = _dot(cm_ref[...], yr) + _dot(sm_ref[...], yi)
    sgn = sgn_ref[...]
    a_nyq = jnp.sum(v * sgn, axis=0, keepdims=True)
    y_ref[0] = y + sgn * (a_nyq * pn_ref[...])


def _hyena_conv(v, cm, sm, p, q, pn, sgn):
    B, L, Ch = v.shape
    full = lambda s: pl.BlockSpec(s, lambda b: (0,) * len(s), pipeline_mode=pl.Buffered(1))
    return pl.pallas_call(
        _hyena_conv_kernel,
        out_shape=jax.ShapeDtypeStruct((B, L, Ch), F32),
        grid=(B,),
        in_specs=[pl.BlockSpec((1, L, Ch), lambda b: (b, 0, 0)), full((L, L)), full((L, L)),
                  full((L, Ch)), full((L, Ch)), full((1, Ch)), full((L, 1))],
        out_specs=pl.BlockSpec((1, L, Ch), lambda b: (b, 0, 0)),
        compiler_params=_cparams(("parallel",)),
        name="hyena_conv",
    )(v, cm, sm, p, q, pn, sgn)


def _moe_kernel(be_ref, nu_ref, x_ref, wgu_ref, bgu_ref, wdn_ref, bdn_ref, y_ref, wgu_bf, wdn_bf, *, d_expert):
    i = pl.program_id(0)
    prev = be_ref[jnp.maximum(i - 1, 0)]
    changed = jnp.logical_or(i == 0, be_ref[i] != prev)

    @pl.when(changed)
    def _():
        wgu_bf[...] = wgu_ref[0].astype(BF16)
        wdn_bf[...] = wdn_ref[0].astype(BF16)

    @pl.when(i < nu_ref[0])
    def _():
        gu = _dot(x_ref[...], wgu_bf[...]) + bgu_ref[0]
        gate = jnp.minimum(gu[:, :d_expert], SWIGLU_LIMIT)
        up = jnp.clip(gu[:, d_expert:], -SWIGLU_LIMIT, SWIGLU_LIMIT)
        glu = gate * jax.nn.sigmoid(gate * SWIGLU_ALPHA)
        act = ((up + 1.0) * glu).astype(BF16)
        y_ref[...] = _dot(act, wdn_bf[...]) + bdn_ref[0]

    @pl.when(i >= nu_ref[0])
    def _():
        y_ref[...] = jnp.zeros_like(y_ref)


def _moe_experts(xs, block_e, n_used, w_gu, b_gu, w_dn, b_dn):
    NP, D = xs.shape
    E, _, F2 = w_gu.shape
    Fe = F2 // 2
    tm = MOE_TILE
    nb = NP // tm
    grid_spec = pltpu.PrefetchScalarGridSpec(
        num_scalar_prefetch=2,
        grid=(nb,),
        in_specs=[
            pl.BlockSpec((tm, D), lambda i, be, nu: (i, 0)),
            pl.BlockSpec((1, D, F2), lambda i, be, nu: (be[i], 0, 0)),
            pl.BlockSpec((1, 1, F2), lambda i, be, nu: (be[i], 0, 0)),
            pl.BlockSpec((1, Fe, D), lambda i, be, nu: (be[i], 0, 0)),
            pl.BlockSpec((1, 1, D), lambda i, be, nu: (be[i], 0, 0)),
        ],
        out_specs=pl.BlockSpec((tm, D), lambda i, be, nu: (i, 0)),
        scratch_shapes=[pltpu.VMEM((D, F2), BF16), pltpu.VMEM((Fe, D), BF16)],
    )
    return pl.pallas_call(
        functools.partial(_moe_kernel, d_expert=Fe),
        out_shape=jax.ShapeDtypeStruct((NP, D), F32),
        grid_spec=grid_spec,
        compiler_params=_cparams(("arbitrary",)),
        name="moe_experts",
    )(block_e, n_used, xs, w_gu, b_gu.reshape(E, 1, F2), w_dn, b_dn.reshape(E, 1, D))


def _combine_kernel(h_ref, yk_ref, rg_ref, g_ref, w_ref, o_ref, *, final):
    D = h_ref.shape[-1]
    rg = rg_ref[0]
    y = yk_ref[0, :, :D] * rg[:, 0:1]
    for k in range(1, TOP_K):
        y += yk_ref[0, :, k * D:(k + 1) * D] * rg[:, k:k + 1]
    h = h_ref[0] + g_ref[0, 0] * y
    if final:
        h = h * lax.rsqrt(jnp.mean(h * h, axis=-1, keepdims=True) + NORM_EPS) * w_ref[...]
    o_ref[0] = h


def _combine(h, yk, route_gates, gate, w, n_ctx_tiles, final):
    B, T, D = h.shape
    tm = TOKEN_TILE
    tok = lambda c: pl.BlockSpec((1, tm, c), lambda b, i: (b, i, 0))
    nseg = gate.shape[1]
    seg = lambda b, i: (b, jnp.minimum((i >= n_ctx_tiles).astype(jnp.int32), nseg - 1), 0, 0)
    return pl.pallas_call(
        functools.partial(_combine_kernel, final=final),
        out_shape=jax.ShapeDtypeStruct((B, T, D), F32),
        grid=(B, T // tm),
        in_specs=[tok(D), tok(TOP_K * D), tok(TOP_K), pl.BlockSpec((1, 1, 1, D), seg),
                  pl.BlockSpec((1, D), lambda b, i: (0, 0))],
        out_specs=tok(D),
        compiler_params=_cparams(("parallel", "parallel")),
        name="combine",
    )(h, yk, route_gates, gate, w)


def _dwconv(u, w):
    K = w.shape[0]
    L = u.shape[1]
    up = jnp.pad(u, ((0, 0), (K // 2, K // 2), (0, 0)))
    return sum(up[:, j:j + L] * w[j] for j in range(K))


def _dft_mats(L):
    n2 = 2 * L
    f = lax.broadcasted_iota(jnp.int32, (L, L), 0)
    t = lax.broadcasted_iota(jnp.int32, (L, L), 1)
    ang = ((f * t) % n2).astype(F32) * (2.0 * math.pi / n2)
    return jnp.cos(ang), jnp.sin(ang)


def _hyena_filter(L, w1, b1, w2, b2, w3, b3, w4, freq, d_hy):
    pos = jnp.arange(L, dtype=F32)
    t = jnp.linspace(0.0, 1.0, L, dtype=F32)[:, None]
    bands = jnp.linspace(1e-4, HYENA_BANDS - 1, HYENA_BANDS, dtype=F32)
    ang = (2.0 * math.pi / L) * pos[:, None] * bands
    z = jnp.concatenate([t, jnp.cos(ang), -jnp.sin(ang)], axis=-1)
    hdn = jnp.sin(freq * (jnp.dot(z, w1, precision=_HI) + b1))
    hdn = jnp.sin(freq * (jnp.dot(hdn, w2, precision=_HI) + b2))
    hdn = jnp.sin(freq * (jnp.dot(hdn, w3, precision=_HI) + b3))
    h = jnp.dot(hdn, w4, precision=_HI)
    deltas = jnp.linspace(math.log(HYENA_DECAY_TARGET) / HYENA_SLOW_PCT,
                          math.log(HYENA_DECAY_TARGET) / HYENA_FAST_PCT, d_hy, dtype=F32)
    window = jnp.exp(-t * jnp.abs(deltas))
    h_fwd = h[:, :d_hy] * window
    h_bwd = h[:, d_hy:] * window
    k_lo = h_fwd
    k_hi = jnp.concatenate([jnp.zeros((1, d_hy), F32), h_bwd[:0:-1]], axis=0)
    return k_lo, k_hi


def _hyena_mixer(u, short_w, short_b, fparams, skip, d_hy):
    B, L, _ = u.shape
    uc = _dwconv(u, short_w) + short_b
    x0, x1, v = uc[..., :d_hy], uc[..., d_hy:2 * d_hy], uc[..., 2 * d_hy:]
    v = v * x1
    k_lo, k_hi = _hyena_filter(L, *fparams, d_hy)
    cm, sm = _dft_mats(L)
    sgn = jnp.where(jnp.arange(L) % 2 == 0, 1.0, -1.0).astype(F32)[:, None]
    wf = jnp.where(jnp.arange(L) == 0, 1.0, 2.0).astype(F32)[:, None] / (2 * L)
    p = (jnp.dot(cm, k_lo, precision=_HI) + sgn * jnp.dot(cm, k_hi, precision=_HI)) * wf
    q = (jnp.dot(sm, k_lo, precision=_HI) + sgn * jnp.dot(sm, k_hi, precision=_HI)) * wf
    pn = jnp.sum(sgn * (k_lo + k_hi), axis=0, keepdims=True) / (2 * L)
    y = _hyena_conv(v, cm.astype(BF16), sm.astype(BF16), p, q, pn, sgn)
    return (y + v * skip) * x0


def _pad_rows(w, lo, total):
    return jnp.zeros((total, w.shape[1]), w.dtype).at[lo:lo + w.shape[0]].set(w)


def _rwkv_mixer_parts(p, mu, w0, w2, a0, a2, g2, k_k, k_a, r_k, nct):
    row = lambda x: x[None, :]
    lw_w, lw_a = w2.shape[1], a2.shape[1]
    outs = []
    for d in range(2):
        w2p = _pad_rows(w2[d], d * lw_w, 2 * lw_w)
        a2p = _pad_rows(a2[d], d * lw_a, 2 * lw_a)
        outs.append(_rwkv_dir(p, row(mu), w2p, row(w0[d]), a2p, row(a0[d]), g2, row(k_k), row(k_a), row(r_k),
                              nct, rev=bool(d)))
    (o0, bv0, g), (o1, bv1, _) = outs
    return o0, o1, bv0, bv1, g


def _gdn_mixer_parts(p, conv_w, a_log, dt_bias, nct, dg):
    return [_gdn_dir(p, conv_w, _gdn_consts(a_log[d], dt_bias[d], d, dg), nct, rev=bool(d), dg=dg) for d in range(2)]


def _moe(n_bf, logits, w_gu, b_gu, w_dn, b_dn):
    Nt, D = n_bf.shape
    E = w_gu.shape[0]
    tm = MOE_TILE
    top_logit, top_idx = lax.top_k(logits, TOP_K)
    gates = jax.nn.softmax(top_logit, axis=-1)
    A = Nt * TOP_K
    flat_e = top_idx.reshape(A)
    onehot = (flat_e[:, None] == jnp.arange(E)[None, :]).astype(jnp.int32)
    rank = jnp.cumsum(onehot, axis=0) - onehot
    counts = jnp.sum(onehot, axis=0)
    blocks_per = (counts + tm - 1) // tm
    block_end = jnp.cumsum(blocks_per)
    padded_starts = (block_end - blocks_per) * tm
    dest = padded_starts[flat_e] + jnp.sum(rank * onehot, axis=1)
    n_blocks = -(-A // tm) + E
    slot_tok = jnp.full((n_blocks * tm,), Nt, jnp.int32).at[dest].set(jnp.arange(A, dtype=jnp.int32) // TOP_K)
    block_e = jnp.minimum(jnp.searchsorted(block_end, jnp.arange(n_blocks), side='right'), E - 1).astype(jnp.int32)
    n_used = block_end[-1:].astype(jnp.int32)
    x_pad = jnp.concatenate([n_bf, jnp.zeros((1, D), n_bf.dtype)], axis=0)
    xs = x_pad[slot_tok]
    yb = _moe_experts(xs, block_e, n_used, w_gu, b_gu, w_dn, b_dn)
    return yb[dest].reshape(Nt, TOP_K * D), gates


def kernel(x, c, ctx, c_ctx, ada_w, ada_b, norm_mix_w, norm_ffn_w, final_norm_w, w_in, w_out, hy_short_w, hy_short_b, hy_f_w1, hy_f_b1, hy_f_w2, hy_f_b2, hy_f_w3, hy_f_b3, hy_f_w4, hy_f_freq, hy_skip, rw_mu, rw_w0, rw_w2, rw_a0, rw_a2, rw_g2, rw_k_k, rw_k_a, rw_r_k, rw_gn_w, rw_gn_b, gdn_conv_w, gdn_a_log, gdn_dt_bias, gdn_norm_w, moe_router_w, moe_router_b, moe_w_gu, moe_b_gu, moe_w_dn, moe_b_dn):
    B, L, D = x.shape
    Lc = ctx.shape[1]
    depth = ada_w.shape[0]
    d_hy = hy_skip.shape[-1]
    hy_cols = 3 * d_hy
    rw_cols = rw_mu.shape[-1]
    gd_cols = w_in.shape[-1] - hy_cols - rw_cols
    dg = gdn_conv_w.shape[-1] // 3
    assert gdn_a_log.shape[-1] == HEADS and rw_w0.shape[-1] == HEADS * RWKV_HEAD_DIM and dg == HEADS * GDN_HEAD_DIM
    assert Lc % TOKEN_TILE == 0 and L % TOKEN_TILE == 0 and TOKEN_TILE % GRID_W == 0
    splits = (hy_cols, rw_cols, gd_cols)
    nct = Lc // TOKEN_TILE
    row = lambda v: v[None, :]

    h = jnp.concatenate([ctx, x], axis=1)
    s_lat = jax.nn.silu(c)
    s_ctx = jax.nn.silu(c_ctx)
    y_moe = None
    mods = None
    for l in range(depth):
        last = l == depth - 1
        ml = jnp.dot(s_lat, ada_w[l], precision=_HI) + ada_b[l]
        mc = jnp.dot(s_ctx, ada_w[l], precision=_HI) + ada_b[l]
        mods = jnp.stack([jnp.broadcast_to(mc, ml.shape), ml], axis=1).reshape(B, 2, N_MOD, 1, D)
        mod = lambda j: mods[:, :, j]
        p_hy, p_rw, p_gd = _in_proj(h, mod(0), mod(1), row(norm_mix_w[l]), w_in[l].astype(BF16), splits, nct)

        fparams = (hy_f_w1[l], hy_f_b1[l], hy_f_w2[l], hy_f_b2[l], hy_f_w3[l], hy_f_b3[l], hy_f_w4[l], hy_f_freq[l])
        y_hy_l = _hyena_mixer(p_hy[:, Lc:], hy_short_w[l], hy_short_b[l], fparams, hy_skip[l], d_hy)
        if last:
            y_hy_c = jnp.zeros((B, Lc, d_hy), F32)
        else:
            y_hy_c = _hyena_mixer(p_hy[:, :Lc], hy_short_w[l], hy_short_b[l], fparams, hy_skip[l], d_hy)
        y_hy = jnp.concatenate([y_hy_c, y_hy_l], axis=1)
        rw_parts = _rwkv_mixer_parts(p_rw, rw_mu[l], rw_w0[l], rw_w2[l], rw_a0[l], rw_a2[l], rw_g2[l],
                                     rw_k_k[l], rw_k_a[l], rw_r_k[l], nct)
        gd_parts = _gdn_mixer_parts(p_gd, gdn_conv_w[l], gdn_a_log[l], gdn_dt_bias[l], nct, dg)

        h, n_bf, logits = _out_proj(h, y_hy, rw_parts, gd_parts, p_gd, row(rw_gn_w[l]), row(rw_gn_b[l]),
                                    row(jnp.tile(gdn_norm_w[l], HEADS)), w_out[l].astype(BF16), mod(2), mod(3),
                                    mod(4), row(norm_ffn_w[l]), moe_router_w[l], row(moe_router_b[l]), nct)
        if last:
            h, n_bf, logits = h[:, Lc:], n_bf[:, Lc:], logits[:, Lc:]
        Tm = h.shape[1]
        yk, route_gates = _moe(n_bf.reshape(B * Tm, D), logits.reshape(B * Tm, -1),
                               moe_w_gu[l], moe_b_gu[l], moe_w_dn[l], moe_b_dn[l])
        h = _combine(h, yk.reshape(B, Tm, TOP_K * D), route_gates.reshape(B, Tm, TOP_K),
                     mods[:, 1:2, 5] if last else mod(5), row(final_norm_w), 0 if last else nct, final=last)
    return h
```

```python
import functools
import math

import jax
import jax.numpy as jnp
import numpy as np
from jax import lax
from jax.experimental import pallas as pl
from jax.experimental.pallas import tpu as pltpu

F32 = jnp.float32
BF16 = jnp.bfloat16

NORM_EPS = 1e-6
N_MOD = 6
GRID_W = 64
CHUNK = 64
HEADS = 4
HYENA_BANDS = 16
HYENA_DECAY_TARGET = 1e-2
HYENA_FAST_PCT = 0.3
HYENA_SLOW_PCT = 1.5
RWKV_HEAD_DIM = 64
RWKV_GN_EPS = 64e-5
GDN_HEAD_DIM = 128
GDN_CONV_HALO = 8
TOP_K = 4
SWIGLU_ALPHA = 1.702
SWIGLU_LIMIT = 7.0

TOKEN_TILE = 256
MOE_TILE = 256
VMEM_LIMIT = 56 * 1024 * 1024
_HI = lax.Precision.HIGHEST


def _cparams(sem):
    return pltpu.CompilerParams(dimension_semantics=sem, vmem_limit_bytes=VMEM_LIMIT)


def _dot(a, b):
    return jnp.dot(a, b, preferred_element_type=F32)


def _split2(x):
    hi = x.astype(BF16)
    lo = (x - hi.astype(F32)).astype(BF16)
    return hi, lo


def _split3(x):
    hi = x.astype(BF16)
    r1 = x - hi.astype(F32)
    mid = r1.astype(BF16)
    lo = (r1 - mid.astype(F32)).astype(BF16)
    return hi, mid, lo


def _dot3(a, b):
    ah, al = _split2(a)
    bh, bl = _split2(b)
    return _dot(ah, bh) + _dot(al, bh) + _dot(ah, bl)


def _dot_xl(sel_bf, x):
    hi, mid, lo = _split3(x)
    return _dot(sel_bf, hi) + _dot(sel_bf, mid) + _dot(sel_bf, lo)


def _dot_xr(x, sel_bf):
    hi, mid, lo = _split3(x)
    return _dot(hi, sel_bf) + _dot(mid, sel_bf) + _dot(lo, sel_bf)


def _f(mask):
    return jnp.where(mask, 1.0, 0.0).astype(F32)


def _block_diag_mask(rows, cols, rblk, cblk):
    r = lax.broadcasted_iota(jnp.int32, (rows, cols), 0) >> int(math.log2(rblk))
    c = lax.broadcasted_iota(jnp.int32, (rows, cols), 1) >> int(math.log2(cblk))
    return r == c


def _softplus(x):
    return jnp.maximum(x, 0.0) + jnp.log1p(jnp.exp(-jnp.abs(x)))


def _chunk_masks(rev):
    shape = (CHUNK, HEADS * CHUNK)
    i = lax.broadcasted_iota(jnp.int32, shape, 0)
    j = lax.broadcasted_iota(jnp.int32, shape, 1) & (CHUNK - 1)
    strict = (j > i) if rev else (j < i)
    incl = (j >= i) if rev else (j <= i)
    same16 = (i >> 4) == (j >> 4)
    same32 = (i >> 5) == (j >> 5)
    ii = lax.broadcasted_iota(jnp.int32, (CHUNK, CHUNK), 0)
    jj = lax.broadcasted_iota(jnp.int32, (CHUNK, CHUNK), 1)
    tri = _f((jj >= ii) if rev else (jj <= ii)).astype(BF16)
    bdf = _f(_block_diag_mask(HEADS * CHUNK, HEADS * CHUNK, CHUNK, CHUNK))
    return dict(rev=rev, strict=_f(strict), incl=_f(incl), incl_b=incl, eye=_f(i == j), m16=_f(same16),
                m32=_f(jnp.logical_and(same32, jnp.logical_not(same16))), m64=_f(jnp.logical_not(same32)),
                tri=tri, bdf=bdf, bd=bdf.astype(BF16))


def _head_col_mask(rows, cols, rblk, cblk):
    r = lax.broadcasted_iota(jnp.int32, (rows, cols), 0) >> int(math.log2(rblk))
    c = (lax.broadcasted_iota(jnp.int32, (rows, cols), 1) >> int(math.log2(cblk))) & (HEADS - 1)
    return _f(r == c).astype(BF16)


def _mmc(a, b):
    return _dot(a.astype(BF16), b.astype(BF16))


def _bdiag(y, bd):
    return jnp.concatenate([y.astype(BF16)] * HEADS, axis=0) * bd


def _bdiag_t(y, bd):
    return jnp.concatenate([y] * HEADS, axis=0).T.astype(BF16) * bd


def _tri_inv_each(a_list, mk):
    bd = mk['bd']
    mmh = lambda xs, ys: [_mmc(x, _bdiag(y, bd)) for x, y in zip(xs, ys)]
    d = [a * mk['m16'] for a in a_list]
    e = [a * mk['m32'] for a in a_list]
    f = [a * mk['m64'] for a in a_list]
    d2 = mmh(d, d)
    d4 = mmh(d2, d2)
    d8 = mmh(d4, d4)
    t = [mk['eye'] + x for x in d]
    for p in (d2, d4, d8):
        t = [x + y for x, y in zip(t, mmh(t, p))]
    for p in (e, f):
        t = [x + y for x, y in zip(t, mmh(mmh(t, p), t))]
    return t


def _tile_of_step(s, nct, n_tiles, rev):
    if not rev:
        return s
    return jnp.where(s < nct, nct - 1 - s, n_tiles - 1 - (s - nct))


def _in_proj_kernel(h_ref, shift_ref, scale_ref, nw_ref, w_ref, hy_ref, rw_ref, gd_ref, *, splits):
    h = h_ref[0]
    n = h * lax.rsqrt(jnp.mean(h * h, axis=-1, keepdims=True) + NORM_EPS) * nw_ref[...]
    n = (n * (1.0 + scale_ref[0, 0]) + shift_ref[0, 0]).astype(BF16)
    c0, c1, c2 = splits
    hy_ref[0] = _dot(n, w_ref[:, :c0])
    rw_ref[0] = _dot(n, w_ref[:, c0:c0 + c1])
    gd_ref[0] = _dot(n, w_ref[:, c0 + c1:c0 + c1 + c2])


def _in_proj(h, shift, scale, norm_w, w_bf, splits, n_ctx_tiles):
    B, T, D = h.shape
    tm = TOKEN_TILE
    seg = lambda b, i: (b, (i >= n_ctx_tiles).astype(jnp.int32), 0, 0)
    cols = w_bf.shape[1]
    outs = [jax.ShapeDtypeStruct((B, T, c), F32) for c in splits]
    return pl.pallas_call(
        functools.partial(_in_proj_kernel, splits=splits),
        out_shape=outs,
        grid=(B, T // tm),
        in_specs=[
            pl.BlockSpec((1, tm, D), lambda b, i: (b, i, 0)),
            pl.BlockSpec((1, 1, 1, D), seg),
            pl.BlockSpec((1, 1, 1, D), seg),
            pl.BlockSpec((1, D), lambda b, i: (0, 0)),
            pl.BlockSpec((D, cols), lambda b, i: (0, 0)),
        ],
        out_specs=[pl.BlockSpec((1, tm, c), lambda b, i: (b, i, 0)) for c in splits],
        compiler_params=_cparams(("parallel", "parallel")),
        name="in_proj",
    )(h, shift, scale, norm_w, w_bf)


def _rwkv_chunks(r, lw, k, v, a, b, s_ref, mk, order):
    n = len(order)
    hk = r.shape[1]
    rows = [slice(ci * CHUNK, (ci + 1) * CHUNK) for ci in order]
    each = lambda fn, *ls: [fn(*xs) for xs in zip(*ls)]
    pick = lambda x: [x[s] for s in rows]
    r, lw, k, v, a, b = pick(r), pick(lw), pick(k), pick(v), pick(a), pick(b)
    bd, bd2 = mk['bd'], mk['bd2']
    c_all = _dot_xl(mk['tri'], jnp.concatenate(lw, axis=1))
    c = [c_all[:, i * hk:(i + 1) * hk] for i in range(n)]
    end = 0 if mk['rev'] else CHUNK - 1
    c_end = [x[end:end + 1] for x in c]
    rt = each(lambda r_, c_: r_ * jnp.exp(c_), r, c)
    at = each(lambda a_, c_, lw_: a_ * jnp.exp(c_ - lw_), a, c, lw)
    enc = each(lambda c_: jnp.exp(-c_), c)
    bt = each(lambda b_, e_: b_ * e_, b, enc)
    kt = each(lambda k_, e_: k_ * e_, k, enc)
    e2 = each(lambda c_, ce: jnp.exp(ce - c_), c, c_end)
    bb = each(lambda b_, e_: b_ * e_, b, e2)
    kb = each(lambda k_, e_: k_ * e_, k, e2)
    btk = each(lambda bt_, kt_: jnp.concatenate(
        [_bdiag_t(bt_, bd), _bdiag_t(kt_, bd)], axis=1), bt, kt)
    res = each(lambda at_, rt_, w_: _mmc(jnp.concatenate([at_, rt_], axis=0), w_), at, rt, btk)
    hc = HEADS * CHUNK
    a_ab = [x[:CHUNK, :hc] * mk['strict'] for x in res]
    a_ak = [x[:CHUNK, hc:] * mk['strict'] for x in res]
    a_rb = [x[CHUNK:, :hc] * mk['incl'] for x in res]
    a_rk = [x[CHUNK:, hc:] * mk['incl'] for x in res]
    t = _tri_inv_each(a_ab, mk)
    vbd = each(lambda v_: _bdiag(v_, bd), v)
    akv = each(_mmc, a_ak, vbd)
    au = each(lambda t_, at_, akv_: _mmc(t_, _bdiag(jnp.concatenate([at_, akv_], axis=1), bd2)), t, at, akv)
    aubd = each(lambda x: _bdiag(x, bd2), au)
    ru = each(_mmc, a_rb, aubd)
    rh = each(lambda rt_, x: rt_ + x[:, :hk], rt, ru)
    oloc = each(lambda a_, vb, x: _mmc(a_, vb) + x[:, hk:], a_rk, vbd, ru)
    mn = each(lambda bb_, kb_, au_, v_: _mmc(
        jnp.concatenate([bb_, kb_], axis=0).T,
        jnp.concatenate([au_, jnp.concatenate([jnp.zeros_like(v_), v_], axis=1)], axis=0)), bb, kb, au, v)
    mt = each(lambda x, ce: x[:, :hk] * mk['bdf'] + mk['eye_k'] * jnp.exp(ce), mn, c_end)
    nt = [x[:, hk:] * mk['bdf'] for x in mn]
    outs = []
    s = s_ref[...]
    for i in range(n):
        outs.append(_mmc(rh[i], s) + oloc[i])
        s = _dot3(mt[i], s) + nt[i]
    s_ref[...] = s
    return outs


def _rwkv_dir_kernel(prev_ref, cur_ref, next_ref, mu_ref, w2_ref, w0_ref, a2_ref, a0_ref, g2_ref, kk_ref,
                     ka_ref, rk_ref, o_ref, bv_ref, g_ref, buf, s_ref, *, rev, nct, n_tiles, dr):
    step = pl.program_id(1)
    t = _tile_of_step(step, nct, n_tiles, rev)
    tm = TOKEN_TILE
    hw = GRID_W

    @pl.when(step == 0)
    def _():
        s_ref[...] = jnp.zeros_like(s_ref)

    is_ctx = t < nct
    first = jnp.logical_or(t == 0, t == nct)
    last = jnp.logical_or(t == nct - 1, t == n_tiles - 1)
    cur = cur_ref[0]
    buf[0:hw] = jnp.where(first, 0.0, prev_ref[0])
    buf[hw:hw + tm] = cur
    buf[hw + tm:hw + tm + hw] = jnp.where(last, 0.0, next_ref[0])
    cols = cur.shape[1]
    left = buf[hw - 1:hw - 1 + tm]
    right = buf[hw + 1:hw + 1 + tm]
    up = buf[0:tm]
    down = buf[2 * hw:2 * hw + tm]
    col = lax.broadcasted_iota(jnp.int32, (tm, 1), 0) & (hw - 1)
    lane = lax.broadcasted_iota(jnp.int32, (1, cols), 1)
    left_g = jnp.where(col == 0, 0.0, left)
    right_g = jnp.where(col == hw - 1, 0.0, right)
    l4 = lane & 3
    sh_lat = jnp.where(l4 == 0, left_g, jnp.where(l4 == 1, right_g, jnp.where(l4 == 2, up, down)))
    sh_ctx = jnp.where((lane & 1) == 0, left, right)
    sh = jnp.where(is_ctx, sh_ctx, sh_lat)
    m = cur + mu_ref[...] * (sh - cur)

    r = m[:, :dr]
    k = m[:, dr:2 * dr]
    v = m[:, 2 * dr:3 * dr]
    lw_w = w2_ref.shape[0]
    lw_a = a2_ref.shape[0]
    o3 = 3 * dr
    wd = m[:, o3:o3 + lw_w]
    ad = m[:, o3 + lw_w:o3 + lw_w + lw_a]
    gd = m[:, o3 + lw_w + lw_a:]
    lw = -_softplus(-(w0_ref[...] + _dot3(jnp.tanh(wd), w2_ref[...]))) - 0.5
    logw = -jnp.exp(lw)
    a_lr = jax.nn.sigmoid(a0_ref[...] + _dot3(ad, a2_ref[...]))
    ones_bd = _f(_block_diag_mask(dr, dr, RWKV_HEAD_DIM, RWKV_HEAD_DIM)).astype(BF16)
    kx = k * kk_ref[...]
    kk = kx * lax.rsqrt(_dot_xr(kx * kx, ones_bd) + 1e-12)
    k_d = k * (1.0 + (a_lr - 1.0) * ka_ref[...])
    b_d = kk * a_lr
    a_s = -kk
    bv_ref[0] = _dot_xr(r * k_d * rk_ref[...], ones_bd) * v
    g_ref[0] = _dot3(jax.nn.sigmoid(gd), g2_ref[...])

    mk = _chunk_masks(rev)
    mk['eye_k'] = _f(lax.broadcasted_iota(jnp.int32, (dr, dr), 0) == lax.broadcasted_iota(jnp.int32, (dr, dr), 1))
    mk['bd2'] = _head_col_mask(HEADS * CHUNK, 2 * dr, CHUNK, RWKV_HEAD_DIM)
    nch = tm // CHUNK
    order = list(range(nch - 1, -1, -1) if rev else range(nch))
    outs = _rwkv_chunks(r, logw, k_d, v, a_s, b_d, s_ref, mk, order)
    for ci, o in zip(order, outs):
        o_ref[0, ci * CHUNK:(ci + 1) * CHUNK, :] = o


def _rwkv_dir(p, mu, w2p, w0, a2p, a0, g2, k_k, k_a, r_k, nct, rev):
    B, T, cols = p.shape
    tm = TOKEN_TILE
    n_tiles = T // tm
    dr = g2.shape[1]
    hpt = tm // GRID_W
    nhb = T // GRID_W
    tile = lambda s: _tile_of_step(s, nct, n_tiles, rev)
    full = lambda a: pl.BlockSpec(a.shape, lambda b, s: (0,) * a.ndim)
    tok = lambda c: pl.BlockSpec((1, tm, c), lambda b, s: (b, tile(s), 0))
    out_sds = jax.ShapeDtypeStruct((B, T, dr), F32)
    return pl.pallas_call(
        functools.partial(_rwkv_dir_kernel, rev=rev, nct=nct, n_tiles=n_tiles, dr=dr),
        out_shape=[out_sds, out_sds, out_sds],
        grid=(B, n_tiles),
        in_specs=[
            pl.BlockSpec((1, GRID_W, cols), lambda b, s: (b, jnp.maximum(tile(s) * hpt - 1, 0), 0)),
            tok(cols),
            pl.BlockSpec((1, GRID_W, cols), lambda b, s: (b, jnp.minimum((tile(s) + 1) * hpt, nhb - 1), 0)),
            full(mu), full(w2p), full(w0), full(a2p), full(a0), full(g2), full(k_k), full(k_a), full(r_k),
        ],
        out_specs=[tok(dr), tok(dr), tok(dr)],
        scratch_shapes=[pltpu.VMEM((tm + 2 * GRID_W, cols), F32), pltpu.VMEM((dr, dr), F32)],
        compiler_params=_cparams(("parallel", "arbitrary")),
        name="rwkv_rev" if rev else "rwkv_fwd",
    )(p, p, p, mu, w2p, w0, a2p, a0, g2, k_k, k_a, r_k)


def _gdn_chunks(q, k, v, beta_k, g_i, g_k, s_ref, mk, order):
    n = len(order)
    kd = GDN_HEAD_DIM
    hk = HEADS * kd
    hc = HEADS * CHUNK
    rows = [slice(ci * CHUNK, (ci + 1) * CHUNK) for ci in order]
    each = lambda fn, *ls: [fn(*xs) for xs in zip(*ls)]
    pick = lambda x: [x[s] for s in rows]
    q, k, v, beta_k, g_i, g_k = pick(q), pick(k), pick(v), pick(beta_k), pick(g_i), pick(g_k)
    gi_all = _dot_xl(mk['tri'], jnp.concatenate(g_i, axis=1))
    gk_all = _dot_xl(mk['tri'], jnp.concatenate(g_k, axis=1))
    gc_i = [gi_all[:, i * hc:(i + 1) * hc] for i in range(n)]
    gc_k = [gk_all[:, i * hk:(i + 1) * hk] for i in range(n)]
    end = 0 if mk['rev'] else CHUNK - 1
    gl_k = [x[end:end + 1] for x in gc_k]
    incl = mk['incl_b']
    gc_j = each(lambda g: jnp.sum(g * mk['eye'], axis=0, keepdims=True), gc_i)
    decay = each(lambda gi, gj: jnp.where(incl, jnp.exp(jnp.where(incl, gi - gj, 0.0)), 0.0), gc_i, gc_j)
    kb = each(lambda k_, b_: k_ * b_, k, beta_k)
    ktT = each(lambda k_: _bdiag_t(k_, mk['bd_kt']), k)
    res = each(lambda kb_, q_, w_: _mmc(jnp.concatenate([kb_, q_], axis=0), w_), kb, q, ktT)
    m = each(lambda x, d_: x[:CHUNK] * d_ * mk['strict'], res, decay)
    attn = each(lambda x, d_: x[CHUNK:] * d_, res, decay)
    t = _tri_inv_each([-x for x in m], mk)
    egc = each(jnp.exp, gc_k)
    x = each(lambda v_, b_, kb_, e_: jnp.concatenate([v_ * b_, kb_ * e_], axis=1), v, beta_k, kb, egc)
    uw = each(lambda t_, x_: _mmc(t_, _bdiag(x_, mk['bd_x'])), t, x)
    auw = each(lambda a_, x_: _mmc(a_, _bdiag(x_, mk['bd_x'])), attn, uw)
    rh = each(lambda q_, e_, x_: q_ * e_ - x_[:, hk:], q, egc, auw)
    oloc = [x_[:, :hk] for x_ in auw]
    khT = each(lambda k_, gl, gc: jnp.concatenate([k_ * jnp.exp(gl - gc), jnp.zeros_like(k_)], axis=0).T,
               k, gl_k, gc_k)
    zero = jnp.zeros((CHUNK, 2 * kd), F32)
    heads = [slice(h * kd, (h + 1) * kd) for h in range(HEADS)]
    mn = [[_mmc(khT[i][sl], jnp.concatenate(
        [jnp.concatenate([uw[i][:, hk + h * kd:hk + (h + 1) * kd], uw[i][:, sl]], axis=1), zero], axis=0))
        for h, sl in enumerate(heads)] for i in range(n)]
    s = [s_ref[h] for h in range(HEADS)]
    outs = []
    for i in range(n):
        o_h = []
        for h, sl in enumerate(heads):
            o_h.append(_mmc(rh[i][:, sl], s[h]) + oloc[i][:, sl])
            mt = mk['eye_k'] * jnp.exp(gl_k[i][:, sl]) - mn[i][h][:, :kd]
            s[h] = _dot3(mt, s[h]) + mn[i][h][:, kd:]
        outs.append(jnp.concatenate(o_h, axis=1))
    for h in range(HEADS):
        s_ref[h] = s[h]
    return outs


def _gdn_dir_kernel(prev_ref, cur_ref, next_ref, cw_ref, selb_k, selg_i, selg_k, na_i, na_k, dt_i, dt_k,
                    o_ref, buf, s_ref, *, rev, nct, n_tiles, dg):
    step = pl.program_id(1)
    t = _tile_of_step(step, nct, n_tiles, rev)
    tm = TOKEN_TILE
    hl = GDN_CONV_HALO

    @pl.when(step == 0)
    def _():
        s_ref[...] = jnp.zeros_like(s_ref)

    first = jnp.logical_or(t == 0, t == nct)
    last = jnp.logical_or(t == nct - 1, t == n_tiles - 1)
    c3 = 3 * dg
    buf[0:hl] = jnp.where(first, 0.0, prev_ref[0])
    buf[hl:hl + tm] = cur_ref[0, :, :c3]
    buf[hl + tm:hl + tm + hl] = jnp.where(last, 0.0, next_ref[0])
    taps = cw_ref.shape[0]
    acc = None
    for j in range(taps):
        off = hl + j - taps // 2
        term = buf[off:off + tm] * cw_ref[j:j + 1, :]
        acc = term if acc is None else acc + term
    qkv = acc * jax.nn.sigmoid(acc)
    ones_bd = _f(_block_diag_mask(dg, dg, GDN_HEAD_DIM, GDN_HEAD_DIM)).astype(BF16)
    q = qkv[:, :dg]
    k = qkv[:, dg:2 * dg]
    v = qkv[:, 2 * dg:]
    q = q * lax.rsqrt(_dot_xr(q * q, ones_bd) + 1e-12) * (GDN_HEAD_DIM ** -0.5)
    k = k * lax.rsqrt(_dot_xr(k * k, ones_bd) + 1e-12)
    rest = cur_ref[0, :, 4 * dg:]
    beta_k = jax.nn.sigmoid(_dot_xr(rest, selb_k[...]))
    g_i = na_i[...] * _softplus(_dot_xr(rest, selg_i[...]) + dt_i[...])
    g_k = na_k[...] * _softplus(_dot_xr(rest, selg_k[...]) + dt_k[...])

    mk = _chunk_masks(rev)
    kd = GDN_HEAD_DIM
    mk['eye_k'] = _f(lax.broadcasted_iota(jnp.int32, (kd, kd), 0) == lax.broadcasted_iota(jnp.int32, (kd, kd), 1))
    mk['bd_kt'] = _f(_block_diag_mask(dg, HEADS * CHUNK, kd, CHUNK)).astype(BF16)
    mk['bd_x'] = _head_col_mask(HEADS * CHUNK, 2 * dg, CHUNK, kd)
    nch = tm // CHUNK
    order = list(range(nch - 1, -1, -1) if rev else range(nch))
    outs = _gdn_chunks(q, k, v, beta_k, g_i, g_k, s_ref, mk, order)
    for ci, o in zip(order, outs):
        o_ref[0, ci * CHUNK:(ci + 1) * CHUNK, :] = o


def _gdn_dir(p, conv_w, consts, nct, rev, dg):
    B, T, cols = p.shape
    tm = TOKEN_TILE
    n_tiles = T // tm
    hl = GDN_CONV_HALO
    hpt = tm // hl
    nhb = T // hl
    c3 = 3 * dg
    tile = lambda s: _tile_of_step(s, nct, n_tiles, rev)
    full = lambda a: pl.BlockSpec(a.shape, lambda b, s: (0,) * a.ndim)
    return pl.pallas_call(
        functools.partial(_gdn_dir_kernel, rev=rev, nct=nct, n_tiles=n_tiles, dg=dg),
        out_shape=jax.ShapeDtypeStruct((B, T, dg), F32),
        grid=(B, n_tiles),
        in_specs=[
            pl.BlockSpec((1, hl, c3), lambda b, s: (b, jnp.maximum(tile(s) * hpt - 1, 0), 0)),
            pl.BlockSpec((1, tm, cols), lambda b, s: (b, tile(s), 0)),
            pl.BlockSpec((1, hl, c3), lambda b, s: (b, jnp.minimum((tile(s) + 1) * hpt, nhb - 1), 0)),
            full(conv_w)] + [full(a) for a in consts],
        out_specs=pl.BlockSpec((1, tm, dg), lambda b, s: (b, tile(s), 0)),
        scratch_shapes=[pltpu.VMEM((tm + 2 * hl, c3), F32), pltpu.VMEM((HEADS, GDN_HEAD_DIM, GDN_HEAD_DIM), F32)],
        compiler_params=_cparams(("parallel", "arbitrary")),
        name="gdn_rev" if rev else "gdn_fwd",
    )(p, p, p, conv_w, *consts)


def _gdn_consts(a_log_d, dt_bias_d, d, dg):
    H = HEADS
    ncol = 4 * H
    def sel(base, width):
        m = np.zeros((ncol, H * width), np.float32)
        for h in range(H):
            m[base + d * H + h, h * width:(h + 1) * width] = 1.0
        return jnp.asarray(m, BF16)
    neg_a = -jnp.exp(a_log_d)
    exp_i = lambda x: jnp.repeat(x, CHUNK)[None, :]
    exp_k = lambda x: jnp.repeat(x, GDN_HEAD_DIM)[None, :]
    return (sel(0, GDN_HEAD_DIM), sel(2 * H, CHUNK), sel(2 * H, GDN_HEAD_DIM),
            exp_i(neg_a), exp_k(neg_a), exp_i(dt_bias_d), exp_k(dt_bias_d))


def _out_proj_kernel(h_ref, hy_ref, ro0_ref, ro1_ref, bv0_ref, bv1_ref, rg_ref, go0_ref, go1_ref, z_ref,
                     gnw_ref, gnb_ref, gdw_ref, w_ref, gate_ref, shift_ref, scale_ref, nw_ref, rwt_ref, rb_ref,
                     hn_ref, n_ref, lg_ref, *, splits):
    c0, c1, c2 = splits
    o = ro0_ref[0] + ro1_ref[0]
    ones_r = _f(_block_diag_mask(c1, c1, RWKV_HEAD_DIM, RWKV_HEAD_DIM)).astype(BF16)
    mean = _dot_xr(o, ones_r) * (1.0 / RWKV_HEAD_DIM)
    cen = o - mean
    var = _dot_xr(cen * cen, ones_r) * (1.0 / RWKV_HEAD_DIM)
    y_rw = (cen * lax.rsqrt(var + RWKV_GN_EPS) * gnw_ref[...] + gnb_ref[...] + bv0_ref[0] + bv1_ref[0]) * rg_ref[0]
    og = go0_ref[0] + go1_ref[0]
    ones_g = _f(_block_diag_mask(c2, c2, GDN_HEAD_DIM, GDN_HEAD_DIM)).astype(BF16)
    ms = _dot_xr(og * og, ones_g) * (1.0 / GDN_HEAD_DIM)
    z = z_ref[0]
    y_gd = og * lax.rsqrt(ms + NORM_EPS) * gdw_ref[...] * (z * jax.nn.sigmoid(z))

    acc = _dot(hy_ref[0].astype(BF16), w_ref[:c0, :])
    acc += _dot(y_rw.astype(BF16), w_ref[c0:c0 + c1, :])
    acc += _dot(y_gd.astype(BF16), w_ref[c0 + c1:c0 + c1 + c2, :])
    h = h_ref[0] + gate_ref[0, 0] * acc
    hn_ref[0] = h
    n = h * lax.rsqrt(jnp.mean(h * h, axis=-1, keepdims=True) + NORM_EPS) * nw_ref[...]
    n = n * (1.0 + scale_ref[0, 0]) + shift_ref[0, 0]
    n_ref[0] = n.astype(BF16)
    lg_ref[0] = _dot3(n, rwt_ref[...]) + rb_ref[...]


def _out_proj(h, y_hy, rw_parts, gd_parts, p_gd, gn_w, gn_b, gd_w, w_bf, gate, shift, scale, norm_w,
              router_w, router_b, n_ctx_tiles):
    B, T, D = h.shape
    tm = TOKEN_TILE
    E = router_w.shape[1]
    splits = (y_hy.shape[-1], rw_parts[0].shape[-1], gd_parts[0].shape[-1])
    c2 = splits[2]
    seg = lambda b, i: (b, (i >= n_ctx_tiles).astype(jnp.int32), 0, 0)
    tok = lambda c: pl.BlockSpec((1, tm, c), lambda b, i: (b, i, 0))
    full = lambda s: pl.BlockSpec(s, lambda b, i: (0,) * len(s))
    z_spec = pl.BlockSpec((1, tm, c2), lambda b, i: (b, i, 3))
    return pl.pallas_call(
        functools.partial(_out_proj_kernel, splits=splits),
        out_shape=[jax.ShapeDtypeStruct((B, T, D), F32), jax.ShapeDtypeStruct((B, T, D), BF16),
                   jax.ShapeDtypeStruct((B, T, E), F32)],
        grid=(B, T // tm),
        in_specs=[tok(D), tok(splits[0])] + [tok(splits[1])] * 5 + [tok(c2), tok(c2), z_spec,
                  full((1, splits[1])), full((1, splits[1])), full((1, c2)), full(w_bf.shape),
                  pl.BlockSpec((1, 1, 1, D), seg), pl.BlockSpec((1, 1, 1, D), seg),
                  pl.BlockSpec((1, 1, 1, D), seg), full((1, D)), full(router_w.shape), full((1, E))],
        out_specs=[tok(D), tok(D), tok(E)],
        compiler_params=_cparams(("parallel", "parallel")),
        name="out_proj",
    )(h, y_hy, *rw_parts, *gd_parts, p_gd, gn_w, gn_b, gd_w, w_bf, gate, shift, scale, norm_w, router_w, router_b)


def _short_conv(u, w, b):
    L = u.shape[0]
    row = lax.broadcasted_iota(jnp.int32, (L, 1), 0)
    prev = jnp.where(row == 0, 0.0, pltpu.roll(u, 1, axis=0))
    nxt = jnp.where(row == L - 1, 0.0, pltpu.roll(u, L - 1, axis=0))
    return prev * w[0:1] + u * w[1:2] + nxt * w[2:3] + b


def _hyena_segment(u_ref, rows, sw_ref, sb_ref, skip_ref, cm_ref, sm_ref, p_ref, q_ref, pn_ref, d):
    conv = lambda g: _short_conv(u_ref[0, rows, g * d:(g + 1) * d], sw_ref[:, g * d:(g + 1) * d],
                                 sb_ref[:, g * d:(g + 1) * d])
    v = conv(2) * conv(1)
    vb = v.astype(BF16)
    a = _dot(cm_ref[...], vb)
    b = _dot(sm_ref[...], vb)
    p = p_ref[...]
    q = q_ref[...]
    yr = (a * p - b * q).astype(BF16)
    yi = (a * q + b * p).astype(BF16)
    y = _dot(cm_ref[...], yr) + _dot(sm_ref[...], yi)
    L = v.shape[0]
    sgn = jnp.where((lax.broadcasted_iota(jnp.int32, (L, 1), 0) & 1) == 0, 1.0, -1.0)
    a_nyq = jnp.sum(v * sgn, axis=0, keepdims=True)
    y = y + sgn * (a_nyq * pn_ref[...])
    return (y + v * skip_ref[...]) * conv(0)


def _hyena_kernel(u_ref, sw_ref, sb_ref, skip_ref, cm_l, sm_l, p_l, q_l, pn_l, cm_c, sm_c, p_c, q_c, pn_c,
                  y_ref, *, lc, with_ctx, d):
    T = u_ref.shape[1]
    lat = slice(lc, T)
    ctx = slice(0, lc)
    y_ref[0, lat, :] = _hyena_segment(u_ref, lat, sw_ref, sb_ref, skip_ref, cm_l, sm_l, p_l, q_l, pn_l, d)
    if with_ctx:
        y_ref[0, ctx, :] = _hyena_segment(u_ref, ctx, sw_ref, sb_ref, skip_ref, cm_c, sm_c, p_c, q_c, pn_c, d)
    else:
        y_ref[0, ctx, :] = jnp.zeros((lc, d), F32)


def _hyena(u, short_w, short_b, skip, lat_consts, ctx_consts, lc, with_ctx):
    B, T, cols = u.shape
    d = cols // 3
    one = pl.Buffered(1)
    full = lambda a: pl.BlockSpec(a.shape, lambda b: (0,) * a.ndim, pipeline_mode=one)
    consts = (short_w, short_b, skip) + tuple(lat_consts) + tuple(ctx_consts)
    return pl.pallas_call(
        functools.partial(_hyena_kernel, lc=lc, with_ctx=with_ctx, d=d),
        out_shape=jax.ShapeDtypeStruct((B, T, d), F32),
        grid=(B,),
        in_specs=[pl.BlockSpec((1, T, cols), lambda b: (b, 0, 0), pipeline_mode=one)] + [full(a) for a in consts],
        out_specs=pl.BlockSpec((1, T, d), lambda b: (b, 0, 0)),
        compiler_params=_cparams(("parallel",)),
        name="hyena",
    )(u, *consts)


def _moe_kernel(be_ref, nu_ref, x_ref, wgu_ref, bgu_ref, wdn_ref, bdn_ref, y_ref, wgu_bf, wdn_bf, *, d_expert):
    i = pl.program_id(0)
    prev = be_ref[jnp.maximum(i - 1, 0)]
    changed = jnp.logical_or(i == 0, be_ref[i] != prev)

    @pl.when(changed)
    def _():
        wgu_bf[...] = wgu_ref[0].astype(BF16)
        wdn_bf[...] = wdn_ref[0].astype(BF16)

    @pl.when(i < nu_ref[0])
    def _():
        gu = _dot(x_ref[...], wgu_bf[...]) + bgu_ref[0]
        gate = jnp.minimum(gu[:, :d_expert], SWIGLU_LIMIT)
        up = jnp.clip(gu[:, d_expert:], -SWIGLU_LIMIT, SWIGLU_LIMIT)
        glu = gate * jax.nn.sigmoid(gate * SWIGLU_ALPHA)
        act = ((up + 1.0) * glu).astype(BF16)
        y_ref[...] = _dot(act, wdn_bf[...]) + bdn_ref[0]

    @pl.when(i >= nu_ref[0])
    def _():
        y_ref[...] = jnp.zeros_like(y_ref)


def _moe_experts(xs, block_e, n_used, w_gu, b_gu, w_dn, b_dn):
    NP, D = xs.shape
    E, _, F2 = w_gu.shape
    Fe = F2 // 2
    tm = MOE_TILE
    nb = NP // tm
    grid_spec = pltpu.PrefetchScalarGridSpec(
        num_scalar_prefetch=2,
        grid=(nb,),
        in_specs=[
            pl.BlockSpec((tm, D), lambda i, be, nu: (i, 0)),
            pl.BlockSpec((1, D, F2), lambda i, be, nu: (be[i], 0, 0)),
            pl.BlockSpec((1, 1, F2), lambda i, be, nu: (be[i], 0, 0)),
            pl.BlockSpec((1, Fe, D), lambda i, be, nu: (be[i], 0, 0)),
            pl.BlockSpec((1, 1, D), lambda i, be, nu: (be[i], 0, 0)),
        ],
        out_specs=pl.BlockSpec((tm, D), lambda i, be, nu: (i, 0)),
        scratch_shapes=[pltpu.VMEM((D, F2), BF16), pltpu.VMEM((Fe, D), BF16)],
    )
    return pl.pallas_call(
        functools.partial(_moe_kernel, d_expert=Fe),
        out_shape=jax.ShapeDtypeStruct((NP, D), F32),
        grid_spec=grid_spec,
        compiler_params=_cparams(("arbitrary",)),
        name="moe_experts",
    )(block_e, n_used, xs, w_gu, b_gu.reshape(E, 1, F2), w_dn, b_dn.reshape(E, 1, D))


def _combine_kernel(h_ref, yk_ref, rg_ref, g_ref, w_ref, o_ref, *, final):
    rg = rg_ref[0]
    y = yk_ref[0, 0].astype(F32) * rg[:, 0:1]
    for k in range(1, TOP_K):
        y += yk_ref[k, 0].astype(F32) * rg[:, k:k + 1]
    h = h_ref[0] + g_ref[0, 0] * y
    if final:
        h = h * lax.rsqrt(jnp.mean(h * h, axis=-1, keepdims=True) + NORM_EPS) * w_ref[...]
    o_ref[0] = h


def _combine(h, yk, route_gates, gate, w, n_ctx_tiles, final):
    B, T, D = h.shape
    tm = TOKEN_TILE
    tok = lambda c: pl.BlockSpec((1, tm, c), lambda b, i: (b, i, 0))
    yk_spec = pl.BlockSpec((TOP_K, 1, tm, D), lambda b, i: (0, b, i, 0))
    nseg = gate.shape[1]
    seg = lambda b, i: (b, jnp.minimum((i >= n_ctx_tiles).astype(jnp.int32), nseg - 1), 0, 0)
    return pl.pallas_call(
        functools.partial(_combine_kernel, final=final),
        out_shape=jax.ShapeDtypeStruct((B, T, D), F32),
        grid=(B, T // tm),
        in_specs=[tok(D), yk_spec, tok(TOP_K), pl.BlockSpec((1, 1, 1, D), seg),
                  pl.BlockSpec((1, D), lambda b, i: (0, 0))],
        out_specs=tok(D),
        compiler_params=_cparams(("parallel", "parallel")),
        name="combine",
    )(h, yk, route_gates, gate, w)


def _dft_mats(L):
    n2 = 2 * L
    f = lax.broadcasted_iota(jnp.int32, (L, L), 0)
    t = lax.broadcasted_iota(jnp.int32, (L, L), 1)
    ang = ((f * t) % n2).astype(F32) * (2.0 * math.pi / n2)
    return jnp.cos(ang), jnp.sin(ang)


def _hyena_filter(L, w1, b1, w2, b2, w3, b3, w4, freq, d_hy):
    pos = jnp.arange(L, dtype=F32)
    t = jnp.linspace(0.0, 1.0, L, dtype=F32)[:, None]
    bands = jnp.linspace(1e-4, HYENA_BANDS - 1, HYENA_BANDS, dtype=F32)
    ang = (2.0 * math.pi / L) * pos[:, None] * bands
    z = jnp.concatenate([t, jnp.cos(ang), -jnp.sin(ang)], axis=-1)
    hdn = jnp.sin(freq * (jnp.dot(z, w1, precision=_HI) + b1))
    hdn = jnp.sin(freq * (jnp.dot(hdn, w2, precision=_HI) + b2))
    hdn = jnp.sin(freq * (jnp.dot(hdn, w3, precision=_HI) + b3))
    h = jnp.dot(hdn, w4, precision=_HI)
    deltas = jnp.linspace(math.log(HYENA_DECAY_TARGET) / HYENA_SLOW_PCT,
                          math.log(HYENA_DECAY_TARGET) / HYENA_FAST_PCT, d_hy, dtype=F32)
    window = jnp.exp(-t * jnp.abs(deltas))
    h_fwd = h[:, :d_hy] * window
    h_bwd = h[:, d_hy:] * window
    k_lo = h_fwd
    k_hi = jnp.concatenate([jnp.zeros((1, d_hy), F32), h_bwd[:0:-1]], axis=0)
    return k_lo, k_hi


def _hyena_consts(L, fparams, d_hy):
    k_lo, k_hi = _hyena_filter(L, *fparams, d_hy)
    cm, sm = _dft_mats(L)
    sgn = jnp.where(jnp.arange(L) % 2 == 0, 1.0, -1.0).astype(F32)[:, None]
    wf = jnp.where(jnp.arange(L) == 0, 1.0, 2.0).astype(F32)[:, None] / (2 * L)
    p = (jnp.dot(cm, k_lo, precision=_HI) + sgn * jnp.dot(cm, k_hi, precision=_HI)) * wf
    q = (jnp.dot(sm, k_lo, precision=_HI) + sgn * jnp.dot(sm, k_hi, precision=_HI)) * wf
    pn = jnp.sum(sgn * (k_lo + k_hi), axis=0, keepdims=True) / (2 * L)
    return cm.astype(BF16), sm.astype(BF16), p, q, pn


def _pad_rows(w, lo, total):
    return jnp.zeros((total, w.shape[1]), w.dtype).at[lo:lo + w.shape[0]].set(w)


def _rwkv_mixer_parts(p, mu, w0, w2, a0, a2, g2, k_k, k_a, r_k, nct):
    row = lambda x: x[None, :]
    lw_w, lw_a = w2.shape[1], a2.shape[1]
    outs = []
    for d in range(2):
        w2p = _pad_rows(w2[d], d * lw_w, 2 * lw_w)
        a2p = _pad_rows(a2[d], d * lw_a, 2 * lw_a)
        outs.append(_rwkv_dir(p, row(mu), w2p, row(w0[d]), a2p, row(a0[d]), g2, row(k_k), row(k_a), row(r_k),
                              nct, rev=bool(d)))
    (o0, bv0, g), (o1, bv1, _) = outs
    return o0, o1, bv0, bv1, g


def _gdn_mixer_parts(p, conv_w, a_log, dt_bias, nct, dg):
    return [_gdn_dir(p, conv_w, _gdn_consts(a_log[d], dt_bias[d], d, dg), nct, rev=bool(d), dg=dg) for d in range(2)]


def _moe(n_bf, logits, w_gu, b_gu, w_dn, b_dn):
    Nt, D = n_bf.shape
    E = w_gu.shape[0]
    tm = MOE_TILE
    top_logit, top_idx = lax.top_k(logits, TOP_K)
    gates = jax.nn.softmax(top_logit, axis=-1)
    A = Nt * TOP_K
    flat_e = top_idx.T.reshape(A)
    iota = jnp.arange(A, dtype=jnp.int32)
    sorted_e, order = lax.sort((flat_e, iota), num_keys=1)
    _, inv = lax.sort((order, iota), num_keys=1)
    experts = jnp.arange(E, dtype=jnp.int32)
    starts = jnp.searchsorted(sorted_e, experts, side='left').astype(jnp.int32)
    counts = jnp.searchsorted(sorted_e, experts, side='right').astype(jnp.int32) - starts
    blocks_per = (counts + tm - 1) // tm
    block_end = jnp.cumsum(blocks_per)
    padded_starts = (block_end - blocks_per) * tm
    dest = padded_starts[flat_e] + inv - starts[flat_e]
    n_blocks = -(-A // tm) + E
    block_e = jnp.minimum(jnp.searchsorted(block_end, jnp.arange(n_blocks), side='right'), E - 1).astype(jnp.int32)
    n_used = block_end[-1:].astype(jnp.int32)
    slot = jnp.arange(n_blocks * tm, dtype=jnp.int32)
    slot_e = jnp.repeat(block_e, tm)
    slot_rank = slot - padded_starts[slot_e]
    slot_src = order[jnp.minimum(starts[slot_e] + slot_rank, A - 1)]
    slot_tok = jnp.where(slot_rank < counts[slot_e], slot_src % Nt, Nt)
    x_pad = jnp.concatenate([n_bf, jnp.zeros((1, D), n_bf.dtype)], axis=0)
    xs = x_pad[slot_tok]
    yb = _moe_experts(xs, block_e, n_used, w_gu, b_gu, w_dn, b_dn)
    return yb[dest.reshape(TOP_K, Nt)], gates


def kernel(x, c, ctx, c_ctx, ada_w, ada_b, norm_mix_w, norm_ffn_w, final_norm_w, w_in, w_out, hy_short_w, hy_short_b, hy_f_w1, hy_f_b1, hy_f_w2, hy_f_b2, hy_f_w3, hy_f_b3, hy_f_w4, hy_f_freq, hy_skip, rw_mu, rw_w0, rw_w2, rw_a0, rw_a2, rw_g2, rw_k_k, rw_k_a, rw_r_k, rw_gn_w, rw_gn_b, gdn_conv_w, gdn_a_log, gdn_dt_bias, gdn_norm_w, moe_router_w, moe_router_b, moe_w_gu, moe_b_gu, moe_w_dn, moe_b_dn):
    B, L, D = x.shape
    Lc = ctx.shape[1]
    depth = ada_w.shape[0]
    d_hy = hy_skip.shape[-1]
    hy_cols = 3 * d_hy
    rw_cols = rw_mu.shape[-1]
    gd_cols = w_in.shape[-1] - hy_cols - rw_cols
    dg = gdn_conv_w.shape[-1] // 3
    assert gdn_a_log.shape[-1] == HEADS and rw_w0.shape[-1] == HEADS * RWKV_HEAD_DIM and dg == HEADS * GDN_HEAD_DIM
    assert Lc % TOKEN_TILE == 0 and L % TOKEN_TILE == 0 and TOKEN_TILE % GRID_W == 0
    splits = (hy_cols, rw_cols, gd_cols)
    nct = Lc // TOKEN_TILE
    row = lambda v: v[None, :]

    h = jnp.concatenate([ctx, x], axis=1)
    s_lat = jax.nn.silu(c)
    s_ctx = jax.nn.silu(c_ctx)
    y_moe = None
    mods = None
    for l in range(depth):
        last = l == depth - 1
        ml = jnp.dot(s_lat, ada_w[l], precision=_HI) + ada_b[l]
        mc = jnp.dot(s_ctx, ada_w[l], precision=_HI) + ada_b[l]
        mods = jnp.stack([jnp.broadcast_to(mc, ml.shape), ml], axis=1).reshape(B, 2, N_MOD, 1, D)
        mod = lambda j: mods[:, :, j]
        p_hy, p_rw, p_gd = _in_proj(h, mod(0), mod(1), row(norm_mix_w[l]), w_in[l].astype(BF16), splits, nct)

        fparams = (hy_f_w1[l], hy_f_b1[l], hy_f_w2[l], hy_f_b2[l], hy_f_w3[l], hy_f_b3[l], hy_f_w4[l], hy_f_freq[l])
        y_hy = _hyena(p_hy, hy_short_w[l], row(hy_short_b[l]), row(hy_skip[l]), _hyena_consts(L, fparams, d_hy),
                      _hyena_consts(Lc, fparams, d_hy), Lc, with_ctx=not last)
        rw_parts = _rwkv_mixer_parts(p_rw, rw_mu[l], rw_w0[l], rw_w2[l], rw_a0[l], rw_a2[l], rw_g2[l],
                                     rw_k_k[l], rw_k_a[l], rw_r_k[l], nct)
        gd_parts = _gdn_mixer_parts(p_gd, gdn_conv_w[l], gdn_a_log[l], gdn_dt_bias[l], nct, dg)

        h, n_bf, logits = _out_proj(h, y_hy, rw_parts, gd_parts, p_gd, row(rw_gn_w[l]), row(rw_gn_b[l]),
                                    row(jnp.tile(gdn_norm_w[l], HEADS)), w_out[l].astype(BF16), mod(2), mod(3),
                                    mod(4), row(norm_ffn_w[l]), moe_router_w[l], row(moe_router_b[l]), nct)
        if last:
            h, n_bf, logits = h[:, Lc:], n_bf[:, Lc:], logits[:, Lc:]
        Tm = h.shape[1]
        yk, route_gates = _moe(n_bf.reshape(B * Tm, D), logits.reshape(B * Tm, -1),
                               moe_w_gu[l], moe_b_gu[l], moe_w_dn[l], moe_b_dn[l])
        h = _combine(h, yk.reshape(TOP_K, B, Tm, D), route_gates.reshape(B, Tm, TOP_K),
                     mods[:, 1:2, 5] if last else mod(5), row(final_norm_w), 0 if last else nct, final=last)
    return h
```

```python
import functools
import math

import jax
import jax.numpy as jnp
import numpy as np
from jax import lax
from jax.experimental import pallas as pl
from jax.experimental.pallas import tpu as pltpu

F32 = jnp.float32
BF16 = jnp.bfloat16

NORM_EPS = 1e-6
N_MOD = 6
GRID_W = 64
CHUNK = 64
HEADS = 4
HYENA_BANDS = 16
HYENA_DECAY_TARGET = 1e-2
HYENA_FAST_PCT = 0.3
HYENA_SLOW_PCT = 1.5
RWKV_HEAD_DIM = 64
RWKV_GN_EPS = 64e-5
GDN_HEAD_DIM = 128
GDN_CONV_HALO = 8
TOP_K = 4
SWIGLU_ALPHA = 1.702
SWIGLU_LIMIT = 7.0

TOKEN_TILE = 256
MOE_TILE = 256
VMEM_LIMIT = 56 * 1024 * 1024
_HI = lax.Precision.HIGHEST


def _cparams(sem):
    return pltpu.CompilerParams(dimension_semantics=sem, vmem_limit_bytes=VMEM_LIMIT)


def _dot(a, b):
    return jnp.dot(a, b, preferred_element_type=F32)


def _split2(x):
    hi = x.astype(BF16)
    lo = (x - hi.astype(F32)).astype(BF16)
    return hi, lo


def _split3(x):
    hi = x.astype(BF16)
    r1 = x - hi.astype(F32)
    mid = r1.astype(BF16)
    lo = (r1 - mid.astype(F32)).astype(BF16)
    return hi, mid, lo


def _dot3(a, b):
    ah, al = _split2(a)
    bh, bl = _split2(b)
    return _dot(ah, bh) + _dot(al, bh) + _dot(ah, bl)


def _dot_xl(sel_bf, x):
    hi, mid, lo = _split3(x)
    return _dot(sel_bf, hi) + _dot(sel_bf, mid) + _dot(sel_bf, lo)


def _dot_xr(x, sel_bf):
    hi, mid, lo = _split3(x)
    return _dot(hi, sel_bf) + _dot(mid, sel_bf) + _dot(lo, sel_bf)


def _dot_sum(x, ones_bf):
    hi, lo = _split2(x)
    return _dot(hi, ones_bf) + _dot(lo, ones_bf)


def _f(mask):
    return jnp.where(mask, 1.0, 0.0).astype(F32)


def _block_diag_mask(rows, cols, rblk, cblk):
    r = lax.broadcasted_iota(jnp.int32, (rows, cols), 0) >> int(math.log2(rblk))
    c = lax.broadcasted_iota(jnp.int32, (rows, cols), 1) >> int(math.log2(cblk))
    return r == c


def _softplus(x):
    return jnp.maximum(x, 0.0) + jnp.log1p(jnp.exp(-jnp.abs(x)))


def _chunk_masks(rev):
    shape = (CHUNK, HEADS * CHUNK)
    i = lax.broadcasted_iota(jnp.int32, shape, 0)
    j = lax.broadcasted_iota(jnp.int32, shape, 1) & (CHUNK - 1)
    strict = (j > i) if rev else (j < i)
    incl = (j >= i) if rev else (j <= i)
    same16 = (i >> 4) == (j >> 4)
    same32 = (i >> 5) == (j >> 5)
    ii = lax.broadcasted_iota(jnp.int32, (CHUNK, CHUNK), 0)
    jj = lax.broadcasted_iota(jnp.int32, (CHUNK, CHUNK), 1)
    tri = _f((jj >= ii) if rev else (jj <= ii)).astype(BF16)
    bdf = _f(_block_diag_mask(HEADS * CHUNK, HEADS * CHUNK, CHUNK, CHUNK))
    return dict(rev=rev, strict=_f(strict), incl=_f(incl), incl_b=incl, eye=_f(i == j), m16=_f(same16),
                m32=_f(jnp.logical_and(same32, jnp.logical_not(same16))), m64=_f(jnp.logical_not(same32)),
                tri=tri, bdf=bdf, bd=bdf.astype(BF16))


def _head_col_mask(rows, cols, rblk, cblk):
    r = lax.broadcasted_iota(jnp.int32, (rows, cols), 0) >> int(math.log2(rblk))
    c = (lax.broadcasted_iota(jnp.int32, (rows, cols), 1) >> int(math.log2(cblk))) & (HEADS - 1)
    return _f(r == c).astype(BF16)


def _mmc(a, b):
    return _dot(a.astype(BF16), b.astype(BF16))


def _bdiag(y, bd):
    return jnp.concatenate([y.astype(BF16)] * HEADS, axis=0) * bd


def _bdiag_t(y, bd):
    return jnp.concatenate([y] * HEADS, axis=0).T.astype(BF16) * bd


def _tri_inv_each(a_list, mk):
    bd = mk['bd']
    mmh = lambda xs, ys: [_mmc(x, _bdiag(y, bd)) for x, y in zip(xs, ys)]
    d = [a * mk['m16'] for a in a_list]
    e = [a * mk['m32'] for a in a_list]
    f = [a * mk['m64'] for a in a_list]
    d2 = mmh(d, d)
    d4 = mmh(d2, d2)
    d8 = mmh(d4, d4)
    t = [mk['eye'] + x for x in d]
    for p in (d2, d4, d8):
        t = [x + y for x, y in zip(t, mmh(t, p))]
    for p in (e, f):
        t = [x + y for x, y in zip(t, mmh(mmh(t, p), t))]
    return t


def _tile_of_step(s, nct, n_tiles, rev):
    if not rev:
        return s
    return jnp.where(s < nct, nct - 1 - s, n_tiles - 1 - (s - nct))


def _in_proj_kernel(h_ref, shift_ref, scale_ref, nw_ref, w_ref, hy_ref, rw_ref, gd_ref, *, splits):
    h = h_ref[0]
    n = h * lax.rsqrt(jnp.mean(h * h, axis=-1, keepdims=True) + NORM_EPS) * nw_ref[...]
    n = (n * (1.0 + scale_ref[0, 0]) + shift_ref[0, 0]).astype(BF16)
    c0, c1, c2 = splits
    hy_ref[0] = _dot(n, w_ref[:, :c0])
    rw_ref[0] = _dot(n, w_ref[:, c0:c0 + c1])
    gd_ref[0] = _dot(n, w_ref[:, c0 + c1:c0 + c1 + c2])


def _in_proj(h, shift, scale, norm_w, w_bf, splits, n_ctx_tiles):
    B, T, D = h.shape
    tm = TOKEN_TILE
    seg = lambda b, i: (b, (i >= n_ctx_tiles).astype(jnp.int32), 0, 0)
    cols = w_bf.shape[1]
    outs = [jax.ShapeDtypeStruct((B, T, c), F32) for c in splits]
    return pl.pallas_call(
        functools.partial(_in_proj_kernel, splits=splits),
        out_shape=outs,
        grid=(B, T // tm),
        in_specs=[
            pl.BlockSpec((1, tm, D), lambda b, i: (b, i, 0)),
            pl.BlockSpec((1, 1, 1, D), seg),
            pl.BlockSpec((1, 1, 1, D), seg),
            pl.BlockSpec((1, D), lambda b, i: (0, 0)),
            pl.BlockSpec((D, cols), lambda b, i: (0, 0)),
        ],
        out_specs=[pl.BlockSpec((1, tm, c), lambda b, i: (b, i, 0)) for c in splits],
        compiler_params=_cparams(("parallel", "parallel")),
        name="in_proj",
    )(h, shift, scale, norm_w, w_bf)


def _rwkv_chunks(r, lw, k, v, a, b, s_ref, mk, order):
    n = len(order)
    hk = r.shape[1]
    rows = [slice(ci * CHUNK, (ci + 1) * CHUNK) for ci in order]
    each = lambda fn, *ls: [fn(*xs) for xs in zip(*ls)]
    pick = lambda x: [x[s] for s in rows]
    r, lw, k, v, a, b = pick(r), pick(lw), pick(k), pick(v), pick(a), pick(b)
    bd, bd2 = mk['bd'], mk['bd2']
    c_all = _dot_xl(mk['tri'], jnp.concatenate(lw, axis=1))
    c = [c_all[:, i * hk:(i + 1) * hk] for i in range(n)]
    end = 0 if mk['rev'] else CHUNK - 1
    c_end = [x[end:end + 1] for x in c]
    rt = each(lambda r_, c_: r_ * jnp.exp(c_), r, c)
    at = each(lambda a_, c_, lw_: a_ * jnp.exp(c_ - lw_), a, c, lw)
    enc = each(lambda c_: jnp.exp(-c_), c)
    bt = each(lambda b_, e_: b_ * e_, b, enc)
    kt = each(lambda k_, e_: k_ * e_, k, enc)
    e2 = each(lambda c_, ce: jnp.exp(ce - c_), c, c_end)
    bb = each(lambda b_, e_: b_ * e_, b, e2)
    kb = each(lambda k_, e_: k_ * e_, k, e2)
    btk = each(lambda bt_, kt_: jnp.concatenate(
        [_bdiag_t(bt_, bd), _bdiag_t(kt_, bd)], axis=1), bt, kt)
    res = each(lambda at_, rt_, w_: _mmc(jnp.concatenate([at_, rt_], axis=0), w_), at, rt, btk)
    hc = HEADS * CHUNK
    a_ab = [x[:CHUNK, :hc] * mk['strict'] for x in res]
    a_ak = [x[:CHUNK, hc:] * mk['strict'] for x in res]
    a_rb = [x[CHUNK:, :hc] * mk['incl'] for x in res]
    a_rk = [x[CHUNK:, hc:] * mk['incl'] for x in res]
    t = _tri_inv_each(a_ab, mk)
    vbd = each(lambda v_: _bdiag(v_, bd), v)
    akv = each(_mmc, a_ak, vbd)
    au = each(lambda t_, at_, akv_: _mmc(t_, _bdiag(jnp.concatenate([at_, akv_], axis=1), bd2)), t, at, akv)
    aubd = each(lambda x: _bdiag(x, bd2), au)
    ru = each(_mmc, a_rb, aubd)
    rh = each(lambda rt_, x: rt_ + x[:, :hk], rt, ru)
    oloc = each(lambda a_, vb, x: _mmc(a_, vb) + x[:, hk:], a_rk, vbd, ru)
    mn = each(lambda bb_, kb_, au_, v_: _mmc(
        jnp.concatenate([bb_, kb_], axis=0).T,
        jnp.concatenate([au_, jnp.concatenate([jnp.zeros_like(v_), v_], axis=1)], axis=0)), bb, kb, au, v)
    mt = each(lambda x, ce: x[:, :hk] * mk['bdf'] + mk['eye_k'] * jnp.exp(ce), mn, c_end)
    nt = [x[:, hk:] * mk['bdf'] for x in mn]
    outs = []
    s = s_ref[...]
    for i in range(n):
        outs.append(_mmc(rh[i], s) + oloc[i])
        s = _dot3(mt[i], s) + nt[i]
    s_ref[...] = s
    return outs


def _rwkv_dir_kernel(prev_ref, cur_ref, next_ref, mu_ref, w2_ref, w0_ref, a2_ref, a0_ref, g2_ref, kk_ref,
                     ka_ref, rk_ref, o_ref, bv_ref, *rest, rev, nct, n_tiles, dr):
    g_ref = rest[0] if len(rest) == 3 else None
    buf, s_ref = rest[-2:]
    step = pl.program_id(1)
    t = _tile_of_step(step, nct, n_tiles, rev)
    tm = TOKEN_TILE
    hw = GRID_W

    @pl.when(step == 0)
    def _():
        s_ref[...] = jnp.zeros_like(s_ref)

    is_ctx = t < nct
    first = jnp.logical_or(t == 0, t == nct)
    last = jnp.logical_or(t == nct - 1, t == n_tiles - 1)
    cur = cur_ref[0]
    buf[0:hw] = jnp.where(first, 0.0, prev_ref[0])
    buf[hw:hw + tm] = cur
    buf[hw + tm:hw + tm + hw] = jnp.where(last, 0.0, next_ref[0])
    cols = cur.shape[1]
    left = buf[hw - 1:hw - 1 + tm]
    right = buf[hw + 1:hw + 1 + tm]
    up = buf[0:tm]
    down = buf[2 * hw:2 * hw + tm]
    col = lax.broadcasted_iota(jnp.int32, (tm, 1), 0) & (hw - 1)
    lane = lax.broadcasted_iota(jnp.int32, (1, cols), 1)
    left_g = jnp.where(col == 0, 0.0, left)
    right_g = jnp.where(col == hw - 1, 0.0, right)
    l4 = lane & 3
    sh_lat = jnp.where(l4 == 0, left_g, jnp.where(l4 == 1, right_g, jnp.where(l4 == 2, up, down)))
    sh_ctx = jnp.where((lane & 1) == 0, left, right)
    sh = jnp.where(is_ctx, sh_ctx, sh_lat)
    m = cur + mu_ref[...] * (sh - cur)

    r = m[:, :dr]
    k = m[:, dr:2 * dr]
    v = m[:, 2 * dr:3 * dr]
    lw_w = w2_ref.shape[0]
    lw_a = a2_ref.shape[0]
    o3 = 3 * dr
    wd = m[:, o3:o3 + lw_w]
    ad = m[:, o3 + lw_w:o3 + lw_w + lw_a]
    gd = m[:, o3 + lw_w + lw_a:]
    lw = -_softplus(-(w0_ref[...] + _dot3(jnp.tanh(wd), w2_ref[...]))) - 0.5
    logw = -jnp.exp(lw)
    a_lr = jax.nn.sigmoid(a0_ref[...] + _dot3(ad, a2_ref[...]))
    ones_bd = _f(_block_diag_mask(dr, dr, RWKV_HEAD_DIM, RWKV_HEAD_DIM)).astype(BF16)
    kx = k * kk_ref[...]
    kk = kx * lax.rsqrt(_dot_sum(kx * kx, ones_bd) + 1e-12)
    k_d = k * (1.0 + (a_lr - 1.0) * ka_ref[...])
    b_d = kk * a_lr
    a_s = -kk
    bv_ref[0] = _dot_sum(r * k_d * rk_ref[...], ones_bd) * v
    if g_ref is not None:
        g_ref[0] = _dot3(jax.nn.sigmoid(gd), g2_ref[...])

    mk = _chunk_masks(rev)
    mk['eye_k'] = _f(lax.broadcasted_iota(jnp.int32, (dr, dr), 0) == lax.broadcasted_iota(jnp.int32, (dr, dr), 1))
    mk['bd2'] = _head_col_mask(HEADS * CHUNK, 2 * dr, CHUNK, RWKV_HEAD_DIM)
    nch = tm // CHUNK
    order = list(range(nch - 1, -1, -1) if rev else range(nch))
    outs = _rwkv_chunks(r, logw, k_d, v, a_s, b_d, s_ref, mk, order)
    for ci, o in zip(order, outs):
        o_ref[0, ci * CHUNK:(ci + 1) * CHUNK, :] = o


def _rwkv_dir(p, mu, w2p, w0, a2p, a0, g2, k_k, k_a, r_k, nct, rev):
    B, T, cols = p.shape
    tm = TOKEN_TILE
    n_tiles = T // tm
    dr = g2.shape[1]
    hpt = tm // GRID_W
    nhb = T // GRID_W
    tile = lambda s: _tile_of_step(s, nct, n_tiles, rev)
    full = lambda a: pl.BlockSpec(a.shape, lambda b, s: (0,) * a.ndim)
    tok = lambda c: pl.BlockSpec((1, tm, c), lambda b, s: (b, tile(s), 0))
    out_sds = jax.ShapeDtypeStruct((B, T, dr), F32)
    n_out = 2 if rev else 3
    return pl.pallas_call(
        functools.partial(_rwkv_dir_kernel, rev=rev, nct=nct, n_tiles=n_tiles, dr=dr),
        out_shape=[out_sds] * n_out,
        grid=(B, n_tiles),
        in_specs=[
            pl.BlockSpec((1, GRID_W, cols), lambda b, s: (b, jnp.maximum(tile(s) * hpt - 1, 0), 0)),
            tok(cols),
            pl.BlockSpec((1, GRID_W, cols), lambda b, s: (b, jnp.minimum((tile(s) + 1) * hpt, nhb - 1), 0)),
            full(mu), full(w2p), full(w0), full(a2p), full(a0), full(g2), full(k_k), full(k_a), full(r_k),
        ],
        out_specs=[tok(dr)] * n_out,
        scratch_shapes=[pltpu.VMEM((tm + 2 * GRID_W, cols), F32), pltpu.VMEM((dr, dr), F32)],
        compiler_params=_cparams(("parallel", "arbitrary")),
        name="rwkv_rev" if rev else "rwkv_fwd",
    )(p, p, p, mu, w2p, w0, a2p, a0, g2, k_k, k_a, r_k)


def _gdn_chunks(q, k, v, beta_k, g_i, g_k, s_ref, mk, order):
    n = len(order)
    kd = GDN_HEAD_DIM
    hk = HEADS * kd
    hc = HEADS * CHUNK
    rows = [slice(ci * CHUNK, (ci + 1) * CHUNK) for ci in order]
    each = lambda fn, *ls: [fn(*xs) for xs in zip(*ls)]
    pick = lambda x: [x[s] for s in rows]
    q, k, v, beta_k, g_i, g_k = pick(q), pick(k), pick(v), pick(beta_k), pick(g_i), pick(g_k)
    gi_all = _dot_xl(mk['tri'], jnp.concatenate(g_i, axis=1))
    gk_all = _dot_xl(mk['tri'], jnp.concatenate(g_k, axis=1))
    gc_i = [gi_all[:, i * hc:(i + 1) * hc] for i in range(n)]
    gc_k = [gk_all[:, i * hk:(i + 1) * hk] for i in range(n)]
    end = 0 if mk['rev'] else CHUNK - 1
    gl_k = [x[end:end + 1] for x in gc_k]
    incl = mk['incl_b']
    gc_j = each(lambda g: jnp.sum(g * mk['eye'], axis=0, keepdims=True), gc_i)
    decay = each(lambda gi, gj: jnp.where(incl, jnp.exp(jnp.where(incl, gi - gj, 0.0)), 0.0), gc_i, gc_j)
    kb = each(lambda k_, b_: k_ * b_, k, beta_k)
    ktT = each(lambda k_: _bdiag_t(k_, mk['bd_kt']), k)
    res = each(lambda kb_, q_, w_: _mmc(jnp.concatenate([kb_, q_], axis=0), w_), kb, q, ktT)
    m = each(lambda x, d_: x[:CHUNK] * d_ * mk['strict'], res, decay)
    attn = each(lambda x, d_: x[CHUNK:] * d_, res, decay)
    t = _tri_inv_each([-x for x in m], mk)
    egc = each(jnp.exp, gc_k)
    x = each(lambda v_, b_, kb_, e_: jnp.concatenate([v_ * b_, kb_ * e_], axis=1), v, beta_k, kb, egc)
    uw = each(lambda t_, x_: _mmc(t_, _bdiag(x_, mk['bd_x'])), t, x)
    auw = each(lambda a_, x_: _mmc(a_, _bdiag(x_, mk['bd_x'])), attn, uw)
    rh = each(lambda q_, e_, x_: q_ * e_ - x_[:, hk:], q, egc, auw)
    oloc = [x_[:, :hk] for x_ in auw]
    khT = each(lambda k_, gl, gc: jnp.concatenate([k_ * jnp.exp(gl - gc), jnp.zeros_like(k_)], axis=0).T,
               k, gl_k, gc_k)
    zero = jnp.zeros((CHUNK, 2 * kd), F32)
    heads = [slice(h * kd, (h + 1) * kd) for h in range(HEADS)]
    mn = [[_mmc(khT[i][sl], jnp.concatenate(
        [jnp.concatenate([uw[i][:, hk + h * kd:hk + (h + 1) * kd], uw[i][:, sl]], axis=1), zero], axis=0))
        for h, sl in enumerate(heads)] for i in range(n)]
    s = [s_ref[h] for h in range(HEADS)]
    outs = []
    for i in range(n):
        o_h = []
        for h, sl in enumerate(heads):
            o_h.append(_mmc(rh[i][:, sl], s[h]) + oloc[i][:, sl])
            mt = mk['eye_k'] * jnp.exp(gl_k[i][:, sl]) - mn[i][h][:, :kd]
            s[h] = _dot3(mt, s[h]) + mn[i][h][:, kd:]
        outs.append(jnp.concatenate(o_h, axis=1))
    for h in range(HEADS):
        s_ref[h] = s[h]
    return outs


def _gdn_dir_kernel(prev_ref, cur_ref, next_ref, cw_ref, selb_k, selg_i, selg_k, na_i, na_k, dt_i, dt_k,
                    o_ref, buf, s_ref, *, rev, nct, n_tiles, dg):
    step = pl.program_id(1)
    t = _tile_of_step(step, nct, n_tiles, rev)
    tm = TOKEN_TILE
    hl = GDN_CONV_HALO

    @pl.when(step == 0)
    def _():
        s_ref[...] = jnp.zeros_like(s_ref)

    first = jnp.logical_or(t == 0, t == nct)
    last = jnp.logical_or(t == nct - 1, t == n_tiles - 1)
    c3 = 3 * dg
    buf[0:hl] = jnp.where(first, 0.0, prev_ref[0])
    buf[hl:hl + tm] = cur_ref[0, :, :c3]
    buf[hl + tm:hl + tm + hl] = jnp.where(last, 0.0, next_ref[0])
    taps = cw_ref.shape[0]
    acc = None
    for j in range(taps):
        off = hl + j - taps // 2
        term = buf[off:off + tm] * cw_ref[j:j + 1, :]
        acc = term if acc is None else acc + term
    qkv = acc * jax.nn.sigmoid(acc)
    ones_bd = _f(_block_diag_mask(dg, dg, GDN_HEAD_DIM, GDN_HEAD_DIM)).astype(BF16)
    q = qkv[:, :dg]
    k = qkv[:, dg:2 * dg]
    v = qkv[:, 2 * dg:]
    q = q * lax.rsqrt(_dot_sum(q * q, ones_bd) + 1e-12) * (GDN_HEAD_DIM ** -0.5)
    k = k * lax.rsqrt(_dot_sum(k * k, ones_bd) + 1e-12)
    rest = cur_ref[0, :, 4 * dg:]
    beta_k = jax.nn.sigmoid(_dot_xr(rest, selb_k[...]))
    g_i = na_i[...] * _softplus(_dot_xr(rest, selg_i[...]) + dt_i[...])
    g_k = na_k[...] * _softplus(_dot_xr(rest, selg_k[...]) + dt_k[...])

    mk = _chunk_masks(rev)
    kd = GDN_HEAD_DIM
    mk['eye_k'] = _f(lax.broadcasted_iota(jnp.int32, (kd, kd), 0) == lax.broadcasted_iota(jnp.int32, (kd, kd), 1))
    mk['bd_kt'] = _f(_block_diag_mask(dg, HEADS * CHUNK, kd, CHUNK)).astype(BF16)
    mk['bd_x'] = _head_col_mask(HEADS * CHUNK, 2 * dg, CHUNK, kd)
    nch = tm // CHUNK
    order = list(range(nch - 1, -1, -1) if rev else range(nch))
    outs = _gdn_chunks(q, k, v, beta_k, g_i, g_k, s_ref, mk, order)
    for ci, o in zip(order, outs):
        o_ref[0, ci * CHUNK:(ci + 1) * CHUNK, :] = o


def _gdn_dir(p, conv_w, consts, nct, rev, dg):
    B, T, cols = p.shape
    tm = TOKEN_TILE
    n_tiles = T // tm
    hl = GDN_CONV_HALO
    hpt = tm // hl
    nhb = T // hl
    c3 = 3 * dg
    tile = lambda s: _tile_of_step(s, nct, n_tiles, rev)
    full = lambda a: pl.BlockSpec(a.shape, lambda b, s: (0,) * a.ndim)
    return pl.pallas_call(
        functools.partial(_gdn_dir_kernel, rev=rev, nct=nct, n_tiles=n_tiles, dg=dg),
        out_shape=jax.ShapeDtypeStruct((B, T, dg), F32),
        grid=(B, n_tiles),
        in_specs=[
            pl.BlockSpec((1, hl, c3), lambda b, s: (b, jnp.maximum(tile(s) * hpt - 1, 0), 0)),
            pl.BlockSpec((1, tm, cols), lambda b, s: (b, tile(s), 0)),
            pl.BlockSpec((1, hl, c3), lambda b, s: (b, jnp.minimum((tile(s) + 1) * hpt, nhb - 1), 0)),
            full(conv_w)] + [full(a) for a in consts],
        out_specs=pl.BlockSpec((1, tm, dg), lambda b, s: (b, tile(s), 0)),
        scratch_shapes=[pltpu.VMEM((tm + 2 * hl, c3), F32), pltpu.VMEM((HEADS, GDN_HEAD_DIM, GDN_HEAD_DIM), F32)],
        compiler_params=_cparams(("parallel", "arbitrary")),
        name="gdn_rev" if rev else "gdn_fwd",
    )(p, p, p, conv_w, *consts)


def _gdn_consts(a_log_d, dt_bias_d, d, dg):
    H = HEADS
    ncol = 4 * H
    def sel(base, width):
        m = np.zeros((ncol, H * width), np.float32)
        for h in range(H):
            m[base + d * H + h, h * width:(h + 1) * width] = 1.0
        return jnp.asarray(m, BF16)
    neg_a = -jnp.exp(a_log_d)
    exp_i = lambda x: jnp.repeat(x, CHUNK)[None, :]
    exp_k = lambda x: jnp.repeat(x, GDN_HEAD_DIM)[None, :]
    return (sel(0, GDN_HEAD_DIM), sel(2 * H, CHUNK), sel(2 * H, GDN_HEAD_DIM),
            exp_i(neg_a), exp_k(neg_a), exp_i(dt_bias_d), exp_k(dt_bias_d))


def _out_proj_kernel(h_ref, hy_ref, ro0_ref, ro1_ref, bv0_ref, bv1_ref, rg_ref, go0_ref, go1_ref, z_ref,
                     gnw_ref, gnb_ref, gdw_ref, w_ref, gate_ref, shift_ref, scale_ref, nw_ref, rwt_ref, rb_ref,
                     hn_ref, n_ref, lg_ref, *, splits):
    c0, c1, c2 = splits
    o = ro0_ref[0] + ro1_ref[0]
    ones_r = _f(_block_diag_mask(c1, c1, RWKV_HEAD_DIM, RWKV_HEAD_DIM)).astype(BF16)
    mean = _dot_sum(o, ones_r) * (1.0 / RWKV_HEAD_DIM)
    cen = o - mean
    var = _dot_sum(cen * cen, ones_r) * (1.0 / RWKV_HEAD_DIM)
    y_rw = (cen * lax.rsqrt(var + RWKV_GN_EPS) * gnw_ref[...] + gnb_ref[...] + bv0_ref[0] + bv1_ref[0]) * rg_ref[0]
    og = go0_ref[0] + go1_ref[0]
    ones_g = _f(_block_diag_mask(c2, c2, GDN_HEAD_DIM, GDN_HEAD_DIM)).astype(BF16)
    ms = _dot_sum(og * og, ones_g) * (1.0 / GDN_HEAD_DIM)
    z = z_ref[0]
    y_gd = og * lax.rsqrt(ms + NORM_EPS) * gdw_ref[...] * (z * jax.nn.sigmoid(z))

    acc = _dot(hy_ref[0].astype(BF16), w_ref[:c0, :])
    acc += _dot(y_rw.astype(BF16), w_ref[c0:c0 + c1, :])
    acc += _dot(y_gd.astype(BF16), w_ref[c0 + c1:c0 + c1 + c2, :])
    h = h_ref[0] + gate_ref[0, 0] * acc
    hn_ref[0] = h
    n = h * lax.rsqrt(jnp.mean(h * h, axis=-1, keepdims=True) + NORM_EPS) * nw_ref[...]
    n = n * (1.0 + scale_ref[0, 0]) + shift_ref[0, 0]
    n_ref[0] = n.astype(BF16)
    lg_ref[0] = _dot3(n, rwt_ref[...]) + rb_ref[...]


def _out_proj(h, y_hy, rw_parts, gd_parts, p_gd, gn_w, gn_b, gd_w, w_bf, gate, shift, scale, norm_w,
              router_w, router_b, n_ctx_tiles):
    B, T, D = h.shape
    tm = TOKEN_TILE
    E = router_w.shape[1]
    splits = (y_hy.shape[-1], rw_parts[0].shape[-1], gd_parts[0].shape[-1])
    c2 = splits[2]
    seg = lambda b, i: (b, (i >= n_ctx_tiles).astype(jnp.int32), 0, 0)
    tok = lambda c: pl.BlockSpec((1, tm, c), lambda b, i: (b, i, 0))
    full = lambda s: pl.BlockSpec(s, lambda b, i: (0,) * len(s))
    z_spec = pl.BlockSpec((1, tm, c2), lambda b, i: (b, i, 3))
    return pl.pallas_call(
        functools.partial(_out_proj_kernel, splits=splits),
        out_shape=[jax.ShapeDtypeStruct((B, T, D), F32), jax.ShapeDtypeStruct((B, T, D), BF16),
                   jax.ShapeDtypeStruct((B, T, E), F32)],
        grid=(B, T // tm),
        in_specs=[tok(D), tok(splits[0])] + [tok(splits[1])] * 5 + [tok(c2), tok(c2), z_spec,
                  full((1, splits[1])), full((1, splits[1])), full((1, c2)), full(w_bf.shape),
                  pl.BlockSpec((1, 1, 1, D), seg), pl.BlockSpec((1, 1, 1, D), seg),
                  pl.BlockSpec((1, 1, 1, D), seg), full((1, D)), full(router_w.shape), full((1, E))],
        out_specs=[tok(D), tok(D), tok(E)],
        compiler_params=_cparams(("parallel", "parallel")),
        name="out_proj",
    )(h, y_hy, *rw_parts, *gd_parts, p_gd, gn_w, gn_b, gd_w, w_bf, gate, shift, scale, norm_w, router_w, router_b)


def _short_conv(u, w, b):
    L = u.shape[0]
    row = lax.broadcasted_iota(jnp.int32, (L, 1), 0)
    prev = jnp.where(row == 0, 0.0, pltpu.roll(u, 1, axis=0))
    nxt = jnp.where(row == L - 1, 0.0, pltpu.roll(u, L - 1, axis=0))
    return prev * w[0:1] + u * w[1:2] + nxt * w[2:3] + b


def _hyena_segment(u_ref, rows, sw_ref, sb_ref, skip_ref, cm_ref, sm_ref, p_ref, q_ref, pn_ref, d):
    conv = lambda g: _short_conv(u_ref[0, rows, g * d:(g + 1) * d], sw_ref[:, g * d:(g + 1) * d],
                                 sb_ref[:, g * d:(g + 1) * d])
    v = conv(2) * conv(1)
    vb = v.astype(BF16)
    a = _dot(cm_ref[...], vb)
    b = _dot(sm_ref[...], vb)
    p = p_ref[...]
    q = q_ref[...]
    yr = (a * p - b * q).astype(BF16)
    yi = (a * q + b * p).astype(BF16)
    y = _dot(cm_ref[...], yr) + _dot(sm_ref[...], yi)
    L = v.shape[0]
    sgn = jnp.where((lax.broadcasted_iota(jnp.int32, (L, 1), 0) & 1) == 0, 1.0, -1.0)
    a_nyq = jnp.sum(v * sgn, axis=0, keepdims=True)
    y = y + sgn * (a_nyq * pn_ref[...])
    return (y + v * skip_ref[...]) * conv(0)


def _hyena_kernel(u_ref, sw_ref, sb_ref, skip_ref, cm_l, sm_l, p_l, q_l, pn_l, cm_c, sm_c, p_c, q_c, pn_c,
                  y_ref, *, lc, with_ctx, d):
    T = u_ref.shape[1]
    lat = slice(lc, T)
    ctx = slice(0, lc)
    y_ref[0, lat, :] = _hyena_segment(u_ref, lat, sw_ref, sb_ref, skip_ref, cm_l, sm_l, p_l, q_l, pn_l, d)
    if with_ctx:
        y_ref[0, ctx, :] = _hyena_segment(u_ref, ctx, sw_ref, sb_ref, skip_ref, cm_c, sm_c, p_c, q_c, pn_c, d)
    else:
        y_ref[0, ctx, :] = jnp.zeros((lc, d), F32)


def _hyena(u, short_w, short_b, skip, lat_consts, ctx_consts, lc, with_ctx):
    B, T, cols = u.shape
    d = cols // 3
    one = pl.Buffered(1)
    full = lambda a: pl.BlockSpec(a.shape, lambda b: (0,) * a.ndim, pipeline_mode=one)
    consts = (short_w, short_b, skip) + tuple(lat_consts) + tuple(ctx_consts)
    return pl.pallas_call(
        functools.partial(_hyena_kernel, lc=lc, with_ctx=with_ctx, d=d),
        out_shape=jax.ShapeDtypeStruct((B, T, d), F32),
        grid=(B,),
        in_specs=[pl.BlockSpec((1, T, cols), lambda b: (b, 0, 0), pipeline_mode=one)] + [full(a) for a in consts],
        out_specs=pl.BlockSpec((1, T, d), lambda b: (b, 0, 0)),
        compiler_params=_cparams(("parallel",)),
        name="hyena",
    )(u, *consts)


def _moe_kernel(be_ref, nu_ref, x_ref, wgu_ref, bgu_ref, wdn_ref, bdn_ref, y_ref, wgu_bf, wdn_bf, *, d_expert):
    i = pl.program_id(0)
    prev = be_ref[jnp.maximum(i - 1, 0)]
    changed = jnp.logical_or(i == 0, be_ref[i] != prev)

    @pl.when(changed)
    def _():
        wgu_bf[...] = wgu_ref[0].astype(BF16)
        wdn_bf[...] = wdn_ref[0].astype(BF16)

    @pl.when(i < nu_ref[0])
    def _():
        gu = _dot(x_ref[...], wgu_bf[...]) + bgu_ref[0]
        gate = jnp.minimum(gu[:, :d_expert], SWIGLU_LIMIT)
        up = jnp.clip(gu[:, d_expert:], -SWIGLU_LIMIT, SWIGLU_LIMIT)
        glu = gate * jax.nn.sigmoid(gate * SWIGLU_ALPHA)
        act = ((up + 1.0) * glu).astype(BF16)
        y_ref[...] = _dot(act, wdn_bf[...]) + bdn_ref[0]

    @pl.when(i >= nu_ref[0])
    def _():
        y_ref[...] = jnp.zeros_like(y_ref)


def _moe_experts(xs, block_e, n_used, w_gu, b_gu, w_dn, b_dn):
    NP, D = xs.shape
    E, _, F2 = w_gu.shape
    Fe = F2 // 2
    tm = MOE_TILE
    nb = NP // tm
    grid_spec = pltpu.PrefetchScalarGridSpec(
        num_scalar_prefetch=2,
        grid=(nb,),
        in_specs=[
            pl.BlockSpec((tm, D), lambda i, be, nu: (i, 0)),
            pl.BlockSpec((1, D, F2), lambda i, be, nu: (be[i], 0, 0)),
            pl.BlockSpec((1, 1, F2), lambda i, be, nu: (be[i], 0, 0)),
            pl.BlockSpec((1, Fe, D), lambda i, be, nu: (be[i], 0, 0)),
            pl.BlockSpec((1, 1, D), lambda i, be, nu: (be[i], 0, 0)),
        ],
        out_specs=pl.BlockSpec((tm, D), lambda i, be, nu: (i, 0)),
        scratch_shapes=[pltpu.VMEM((D, F2), BF16), pltpu.VMEM((Fe, D), BF16)],
    )
    return pl.pallas_call(
        functools.partial(_moe_kernel, d_expert=Fe),
        out_shape=jax.ShapeDtypeStruct((NP, D), F32),
        grid_spec=grid_spec,
        compiler_params=_cparams(("arbitrary",)),
        name="moe_experts",
    )(block_e, n_used, xs, w_gu, b_gu.reshape(E, 1, F2), w_dn, b_dn.reshape(E, 1, D))


def _combine_kernel(h_ref, yk_ref, rg_ref, g_ref, w_ref, o_ref, *, final):
    rg = rg_ref[0]
    y = yk_ref[0, 0].astype(F32) * rg[:, 0:1]
    for k in range(1, TOP_K):
        y += yk_ref[k, 0].astype(F32) * rg[:, k:k + 1]
    h = h_ref[0] + g_ref[0, 0] * y
    if final:
        h = h * lax.rsqrt(jnp.mean(h * h, axis=-1, keepdims=True) + NORM_EPS) * w_ref[...]
    o_ref[0] = h


def _combine(h, yk, route_gates, gate, w, n_ctx_tiles, final):
    B, T, D = h.shape
    tm = TOKEN_TILE
    tok = lambda c: pl.BlockSpec((1, tm, c), lambda b, i: (b, i, 0))
    yk_spec = pl.BlockSpec((TOP_K, 1, tm, D), lambda b, i: (0, b, i, 0))
    nseg = gate.shape[1]
    seg = lambda b, i: (b, jnp.minimum((i >= n_ctx_tiles).astype(jnp.int32), nseg - 1), 0, 0)
    return pl.pallas_call(
        functools.partial(_combine_kernel, final=final),
        out_shape=jax.ShapeDtypeStruct((B, T, D), F32),
        grid=(B, T // tm),
        in_specs=[tok(D), yk_spec, tok(TOP_K), pl.BlockSpec((1, 1, 1, D), seg),
                  pl.BlockSpec((1, D), lambda b, i: (0, 0))],
        out_specs=tok(D),
        compiler_params=_cparams(("parallel", "parallel")),
        name="combine",
    )(h, yk, route_gates, gate, w)


def _dft_mats(L):
    n2 = 2 * L
    f = lax.broadcasted_iota(jnp.int32, (L, L), 0)
    t = lax.broadcasted_iota(jnp.int32, (L, L), 1)
    ang = ((f * t) % n2).astype(F32) * (2.0 * math.pi / n2)
    return jnp.cos(ang), jnp.sin(ang)


def _hyena_filter(L, w1, b1, w2, b2, w3, b3, w4, freq, d_hy):
    pos = jnp.arange(L, dtype=F32)
    t = jnp.linspace(0.0, 1.0, L, dtype=F32)[:, None]
    bands = jnp.linspace(1e-4, HYENA_BANDS - 1, HYENA_BANDS, dtype=F32)
    ang = (2.0 * math.pi / L) * pos[:, None] * bands
    z = jnp.concatenate([t, jnp.cos(ang), -jnp.sin(ang)], axis=-1)
    hdn = jnp.sin(freq * (jnp.dot(z, w1, precision=_HI) + b1))
    hdn = jnp.sin(freq * (jnp.dot(hdn, w2, precision=_HI) + b2))
    hdn = jnp.sin(freq * (jnp.dot(hdn, w3, precision=_HI) + b3))
    h = jnp.dot(hdn, w4, precision=_HI)
    deltas = jnp.linspace(math.log(HYENA_DECAY_TARGET) / HYENA_SLOW_PCT,
                          math.log(HYENA_DECAY_TARGET) / HYENA_FAST_PCT, d_hy, dtype=F32)
    window = jnp.exp(-t * jnp.abs(deltas))
    h_fwd = h[:, :d_hy] * window
    h_bwd = h[:, d_hy:] * window
    k_lo = h_fwd
    k_hi = jnp.concatenate([jnp.zeros((1, d_hy), F32), h_bwd[:0:-1]], axis=0)
    return k_lo, k_hi


def _hyena_consts(dft, fparams, d_hy):
    cm, sm = dft
    L = cm.shape[0]
    k_lo, k_hi = _hyena_filter(L, *fparams, d_hy)
    sgn = jnp.where(jnp.arange(L) % 2 == 0, 1.0, -1.0).astype(F32)[:, None]
    wf = jnp.where(jnp.arange(L) == 0, 1.0, 2.0).astype(F32)[:, None] / (2 * L)
    p = (jnp.dot(cm, k_lo, precision=_HI) + sgn * jnp.dot(cm, k_hi, precision=_HI)) * wf
    q = (jnp.dot(sm, k_lo, precision=_HI) + sgn * jnp.dot(sm, k_hi, precision=_HI)) * wf
    pn = jnp.sum(sgn * (k_lo + k_hi), axis=0, keepdims=True) / (2 * L)
    return cm.astype(BF16), sm.astype(BF16), p, q, pn


def _pad_rows(w, lo, total):
    return jnp.zeros((total, w.shape[1]), w.dtype).at[lo:lo + w.shape[0]].set(w)


def _rwkv_mixer_parts(p, mu, w0, w2, a0, a2, g2, k_k, k_a, r_k, nct):
    row = lambda x: x[None, :]
    lw_w, lw_a = w2.shape[1], a2.shape[1]
    outs = []
    for d in range(2):
        w2p = _pad_rows(w2[d], d * lw_w, 2 * lw_w)
        a2p = _pad_rows(a2[d], d * lw_a, 2 * lw_a)
        outs.append(_rwkv_dir(p, row(mu), w2p, row(w0[d]), a2p, row(a0[d]), g2, row(k_k), row(k_a), row(r_k),
                              nct, rev=bool(d)))
    (o0, bv0, g), (o1, bv1) = outs
    return o0, o1, bv0, bv1, g


def _gdn_mixer_parts(p, conv_w, a_log, dt_bias, nct, dg):
    return [_gdn_dir(p, conv_w, _gdn_consts(a_log[d], dt_bias[d], d, dg), nct, rev=bool(d), dg=dg) for d in range(2)]


def _moe(n_bf, logits, w_gu, b_gu, w_dn, b_dn):
    Nt, D = n_bf.shape
    E = w_gu.shape[0]
    tm = MOE_TILE
    top_logit, top_idx = lax.top_k(logits, TOP_K)
    gates = jax.nn.softmax(top_logit, axis=-1)
    A = Nt * TOP_K
    flat_e = top_idx.T.reshape(A)
    experts = jnp.arange(E, dtype=jnp.int32)
    counts = jnp.sum((flat_e[:, None] == experts[None, :]).astype(jnp.int32), axis=0)
    blocks_per = (counts + tm - 1) // tm
    block_end = jnp.cumsum(blocks_per)
    n_blocks = -(-A // tm) + E
    n_slots = n_blocks * tm
    block_e = jnp.minimum(jnp.sum((block_end[None, :] <= jnp.arange(n_blocks)[:, None]).astype(jnp.int32), axis=1),
                          E - 1).astype(jnp.int32)
    n_used = block_end[-1:].astype(jnp.int32)
    need = blocks_per * tm - counts
    pad_keys = jnp.where(jnp.arange(tm - 1)[None, :] < need[:, None], experts[:, None], E).reshape(E * (tm - 1))
    tail = jnp.full((n_slots - A - E * (tm - 1),), E, jnp.int32)
    keys = jnp.concatenate([flat_e, pad_keys, tail])
    ids = jnp.arange(n_slots, dtype=jnp.int32)
    _, slot_id = lax.sort((keys, ids), num_keys=1)
    _, slot_of_id = lax.sort((slot_id, ids), num_keys=1)
    dest = slot_of_id[:A]
    slot_tok = jnp.where(slot_id < A, slot_id % Nt, 0)
    xs = n_bf[slot_tok]
    yb = _moe_experts(xs, block_e, n_used, w_gu, b_gu, w_dn, b_dn)
    return yb[dest.reshape(TOP_K, Nt)], gates


def kernel(x, c, ctx, c_ctx, ada_w, ada_b, norm_mix_w, norm_ffn_w, final_norm_w, w_in, w_out, hy_short_w, hy_short_b, hy_f_w1, hy_f_b1, hy_f_w2, hy_f_b2, hy_f_w3, hy_f_b3, hy_f_w4, hy_f_freq, hy_skip, rw_mu, rw_w0, rw_w2, rw_a0, rw_a2, rw_g2, rw_k_k, rw_k_a, rw_r_k, rw_gn_w, rw_gn_b, gdn_conv_w, gdn_a_log, gdn_dt_bias, gdn_norm_w, moe_router_w, moe_router_b, moe_w_gu, moe_b_gu, moe_w_dn, moe_b_dn):
    B, L, D = x.shape
    Lc = ctx.shape[1]
    depth = ada_w.shape[0]
    d_hy = hy_skip.shape[-1]
    hy_cols = 3 * d_hy
    rw_cols = rw_mu.shape[-1]
    gd_cols = w_in.shape[-1] - hy_cols - rw_cols
    dg = gdn_conv_w.shape[-1] // 3
    assert gdn_a_log.shape[-1] == HEADS and rw_w0.shape[-1] == HEADS * RWKV_HEAD_DIM and dg == HEADS * GDN_HEAD_DIM
    assert Lc % TOKEN_TILE == 0 and L % TOKEN_TILE == 0 and TOKEN_TILE % GRID_W == 0
    splits = (hy_cols, rw_cols, gd_cols)
    nct = Lc // TOKEN_TILE
    row = lambda v: v[None, :]

    h = jnp.concatenate([ctx, x], axis=1)
    s_lat = jax.nn.silu(c)
    s_ctx = jax.nn.silu(c_ctx)
    dft_lat = _dft_mats(L)
    dft_ctx = _dft_mats(Lc)
    mods = None
    for l in range(depth):
        last = l == depth - 1
        ml = jnp.dot(s_lat, ada_w[l], precision=_HI) + ada_b[l]
        mc = jnp.dot(s_ctx, ada_w[l], precision=_HI) + ada_b[l]
        mods = jnp.stack([jnp.broadcast_to(mc, ml.shape), ml], axis=1).reshape(B, 2, N_MOD, 1, D)
        mod = lambda j: mods[:, :, j]
        p_hy, p_rw, p_gd = _in_proj(h, mod(0), mod(1), row(norm_mix_w[l]), w_in[l].astype(BF16), splits, nct)

        fparams = (hy_f_w1[l], hy_f_b1[l], hy_f_w2[l], hy_f_b2[l], hy_f_w3[l], hy_f_b3[l], hy_f_w4[l], hy_f_freq[l])
        y_hy = _hyena(p_hy, hy_short_w[l], row(hy_short_b[l]), row(hy_skip[l]), _hyena_consts(dft_lat, fparams, d_hy),
                      _hyena_consts(dft_ctx, fparams, d_hy), Lc, with_ctx=not last)
        rw_parts = _rwkv_mixer_parts(p_rw, rw_mu[l], rw_w0[l], rw_w2[l], rw_a0[l], rw_a2[l], rw_g2[l],
                                     rw_k_k[l], rw_k_a[l], rw_r_k[l], nct)
        gd_parts = _gdn_mixer_parts(p_gd, gdn_conv_w[l], gdn_a_log[l], gdn_dt_bias[l], nct, dg)

        h, n_bf, logits = _out_proj(h, y_hy, rw_parts, gd_parts, p_gd, row(rw_gn_w[l]), row(rw_gn_b[l]),
                                    row(jnp.tile(gdn_norm_w[l], HEADS)), w_out[l].astype(BF16), mod(2), mod(3),
                                    mod(4), row(norm_ffn_w[l]), moe_router_w[l], row(moe_router_b[l]), nct)
        if last:
            h, n_bf, logits = h[:, Lc:], n_bf[:, Lc:], logits[:, Lc:]
        Tm = h.shape[1]
        yk, route_gates = _moe(n_bf.reshape(B * Tm, D), logits.reshape(B * Tm, -1),
                               moe_w_gu[l], moe_b_gu[l], moe_w_dn[l], moe_b_dn[l])
        h = _combine(h, yk.reshape(TOP_K, B, Tm, D), route_gates.reshape(B, Tm, TOP_K),
                     mods[:, 1:2, 5] if last else mod(5), row(final_norm_w), 0 if last else nct, final=last)
    return h
```

```python
import functools
import math

import jax
import jax.numpy as jnp
import numpy as np
from jax import lax
from jax.experimental import pallas as pl
from jax.experimental.pallas import tpu as pltpu

F32 = jnp.float32
BF16 = jnp.bfloat16

NORM_EPS = 1e-6
N_MOD = 6
GRID_W = 64
CHUNK = 64
HEADS = 4
HYENA_BANDS = 16
HYENA_DECAY_TARGET = 1e-2
HYENA_FAST_PCT = 0.3
HYENA_SLOW_PCT = 1.5
RWKV_HEAD_DIM = 64
RWKV_GN_EPS = 64e-5
GDN_HEAD_DIM = 128
GDN_CONV_HALO = 8
TOP_K = 4
SWIGLU_ALPHA = 1.702
SWIGLU_LIMIT = 7.0

TOKEN_TILE = 256
MOE_TILE = 256
VMEM_LIMIT = 56 * 1024 * 1024
_HI = lax.Precision.HIGHEST


def _cparams(sem):
    return pltpu.CompilerParams(dimension_semantics=sem, vmem_limit_bytes=VMEM_LIMIT)


def _dot(a, b):
    return jnp.dot(a, b, preferred_element_type=F32)


def _split2(x):
    hi = x.astype(BF16)
    lo = (x - hi.astype(F32)).astype(BF16)
    return hi, lo


def _split3(x):
    hi = x.astype(BF16)
    r1 = x - hi.astype(F32)
    mid = r1.astype(BF16)
    lo = (r1 - mid.astype(F32)).astype(BF16)
    return hi, mid, lo


def _dot3(a, b):
    ah, al = _split2(a)
    bh, bl = _split2(b)
    return _dot(ah, bh) + _dot(al, bh) + _dot(ah, bl)


def _dot_xl(sel_bf, x):
    hi, mid, lo = _split3(x)
    return _dot(sel_bf, hi) + _dot(sel_bf, mid) + _dot(sel_bf, lo)


def _dot_xr(x, sel_bf):
    hi, mid, lo = _split3(x)
    return _dot(hi, sel_bf) + _dot(mid, sel_bf) + _dot(lo, sel_bf)


def _dot_sum(x, ones_bf):
    hi, lo = _split2(x)
    return _dot(hi, ones_bf) + _dot(lo, ones_bf)


def _f(mask):
    return jnp.where(mask, 1.0, 0.0).astype(F32)


def _block_diag_mask(rows, cols, rblk, cblk):
    r = lax.broadcasted_iota(jnp.int32, (rows, cols), 0) >> int(math.log2(rblk))
    c = lax.broadcasted_iota(jnp.int32, (rows, cols), 1) >> int(math.log2(cblk))
    return r == c


def _softplus(x):
    return jnp.maximum(x, 0.0) + jnp.log1p(jnp.exp(-jnp.abs(x)))


def _chunk_masks(rev):
    shape = (CHUNK, HEADS * CHUNK)
    i = lax.broadcasted_iota(jnp.int32, shape, 0)
    j = lax.broadcasted_iota(jnp.int32, shape, 1) & (CHUNK - 1)
    strict = (j > i) if rev else (j < i)
    incl = (j >= i) if rev else (j <= i)
    same16 = (i >> 4) == (j >> 4)
    same32 = (i >> 5) == (j >> 5)
    ii = lax.broadcasted_iota(jnp.int32, (CHUNK, CHUNK), 0)
    jj = lax.broadcasted_iota(jnp.int32, (CHUNK, CHUNK), 1)
    tri = _f((jj >= ii) if rev else (jj <= ii)).astype(BF16)
    bdf = _f(_block_diag_mask(HEADS * CHUNK, HEADS * CHUNK, CHUNK, CHUNK))
    return dict(rev=rev, strict=_f(strict), incl=_f(incl), incl_b=incl, eye=_f(i == j), m16=_f(same16),
                m32=_f(jnp.logical_and(same32, jnp.logical_not(same16))), m64=_f(jnp.logical_not(same32)),
                tri=tri, bdf=bdf, bd=bdf.astype(BF16))


def _head_col_mask(rows, cols, rblk, cblk):
    r = lax.broadcasted_iota(jnp.int32, (rows, cols), 0) >> int(math.log2(rblk))
    c = (lax.broadcasted_iota(jnp.int32, (rows, cols), 1) >> int(math.log2(cblk))) & (HEADS - 1)
    return _f(r == c).astype(BF16)


def _mmc(a, b):
    return _dot(a.astype(BF16), b.astype(BF16))


def _bdiag(y, bd):
    return jnp.concatenate([y.astype(BF16)] * HEADS, axis=0) * bd


def _bdiag_t(y, bd):
    return jnp.concatenate([y] * HEADS, axis=0).T.astype(BF16) * bd


def _tri_inv_each(a_list, mk):
    bd = mk['bd']
    mmh = lambda xs, ys: [_mmc(x, _bdiag(y, bd)) for x, y in zip(xs, ys)]
    d = [a * mk['m16'] for a in a_list]
    e = [a * mk['m32'] for a in a_list]
    f = [a * mk['m64'] for a in a_list]
    d2 = mmh(d, d)
    d4 = mmh(d2, d2)
    d8 = mmh(d4, d4)
    t = [mk['eye'] + x for x in d]
    for p in (d2, d4, d8):
        t = [x + y for x, y in zip(t, mmh(t, p))]
    for p in (e, f):
        t = [x + y for x, y in zip(t, mmh(mmh(t, p), t))]
    return t


def _tile_of_step(s, nct, n_tiles, rev):
    if not rev:
        return s
    return jnp.where(s < nct, nct - 1 - s, n_tiles - 1 - (s - nct))


def _in_proj_kernel(h_ref, shift_ref, scale_ref, nw_ref, w_ref, hy_ref, rw_ref, gd_ref, *, splits):
    h = h_ref[0]
    n = h * lax.rsqrt(jnp.mean(h * h, axis=-1, keepdims=True) + NORM_EPS) * nw_ref[...]
    n = (n * (1.0 + scale_ref[0, 0]) + shift_ref[0, 0]).astype(BF16)
    c0, c1, c2 = splits
    hy_ref[0] = _dot(n, w_ref[:, :c0])
    rw_ref[0] = _dot(n, w_ref[:, c0:c0 + c1])
    gd_ref[0] = _dot(n, w_ref[:, c0 + c1:c0 + c1 + c2])


def _in_proj(h, shift, scale, norm_w, w_bf, splits, n_ctx_tiles):
    B, T, D = h.shape
    tm = TOKEN_TILE
    seg = lambda b, i: (b, (i >= n_ctx_tiles).astype(jnp.int32), 0, 0)
    cols = w_bf.shape[1]
    outs = [jax.ShapeDtypeStruct((B, T, c), F32) for c in splits]
    return pl.pallas_call(
        functools.partial(_in_proj_kernel, splits=splits),
        out_shape=outs,
        grid=(B, T // tm),
        in_specs=[
            pl.BlockSpec((1, tm, D), lambda b, i: (b, i, 0)),
            pl.BlockSpec((1, 1, 1, D), seg),
            pl.BlockSpec((1, 1, 1, D), seg),
            pl.BlockSpec((1, D), lambda b, i: (0, 0)),
            pl.BlockSpec((D, cols), lambda b, i: (0, 0)),
        ],
        out_specs=[pl.BlockSpec((1, tm, c), lambda b, i: (b, i, 0)) for c in splits],
        compiler_params=_cparams(("parallel", "parallel")),
        name="in_proj",
    )(h, shift, scale, norm_w, w_bf)


def _rwkv_chunks(r, lw, k, v, a, b, s_ref, mk, order):
    n = len(order)
    hk = r.shape[1]
    rows = [slice(ci * CHUNK, (ci + 1) * CHUNK) for ci in order]
    each = lambda fn, *ls: [fn(*xs) for xs in zip(*ls)]
    pick = lambda x: [x[s] for s in rows]
    r, lw, k, v, a, b = pick(r), pick(lw), pick(k), pick(v), pick(a), pick(b)
    bd, bd2 = mk['bd'], mk['bd2']
    c_all = _dot_xl(mk['tri'], jnp.concatenate(lw, axis=1))
    c = [c_all[:, i * hk:(i + 1) * hk] for i in range(n)]
    end = 0 if mk['rev'] else CHUNK - 1
    c_end = [x[end:end + 1] for x in c]
    rt = each(lambda r_, c_: r_ * jnp.exp(c_), r, c)
    at = each(lambda a_, c_, lw_: a_ * jnp.exp(c_ - lw_), a, c, lw)
    enc = each(lambda c_: jnp.exp(-c_), c)
    bt = each(lambda b_, e_: b_ * e_, b, enc)
    kt = each(lambda k_, e_: k_ * e_, k, enc)
    e2 = each(lambda c_, ce: jnp.exp(ce - c_), c, c_end)
    bb = each(lambda b_, e_: b_ * e_, b, e2)
    kb = each(lambda k_, e_: k_ * e_, k, e2)
    btk = each(lambda bt_, kt_: jnp.concatenate(
        [_bdiag_t(bt_, bd), _bdiag_t(kt_, bd)], axis=1), bt, kt)
    res = each(lambda at_, rt_, w_: _mmc(jnp.concatenate([at_, rt_], axis=0), w_), at, rt, btk)
    hc = HEADS * CHUNK
    a_ab = [x[:CHUNK, :hc] * mk['strict'] for x in res]
    a_ak = [x[:CHUNK, hc:] * mk['strict'] for x in res]
    a_rb = [x[CHUNK:, :hc] * mk['incl'] for x in res]
    a_rk = [x[CHUNK:, hc:] * mk['incl'] for x in res]
    t = _tri_inv_each(a_ab, mk)
    vbd = each(lambda v_: _bdiag(v_, bd), v)
    akv = each(_mmc, a_ak, vbd)
    au = each(lambda t_, at_, akv_: _mmc(t_, _bdiag(jnp.concatenate([at_, akv_], axis=1), bd2)), t, at, akv)
    aubd = each(lambda x: _bdiag(x, bd2), au)
    ru = each(_mmc, a_rb, aubd)
    rh = each(lambda rt_, x: rt_ + x[:, :hk], rt, ru)
    oloc = each(lambda a_, vb, x: _mmc(a_, vb) + x[:, hk:], a_rk, vbd, ru)
    mn = each(lambda bb_, kb_, au_, v_: _mmc(
        jnp.concatenate([bb_, kb_], axis=0).T,
        jnp.concatenate([au_, jnp.concatenate([jnp.zeros_like(v_), v_], axis=1)], axis=0)), bb, kb, au, v)
    mt = each(lambda x, ce: x[:, :hk] * mk['bdf'] + mk['eye_k'] * jnp.exp(ce), mn, c_end)
    nt = [x[:, hk:] * mk['bdf'] for x in mn]
    outs = []
    s = s_ref[...]
    for i in range(n):
        outs.append(_mmc(rh[i], s) + oloc[i])
        s = _dot3(mt[i], s) + nt[i]
    s_ref[...] = s
    return outs


def _rwkv_dir_kernel(prev_ref, cur_ref, next_ref, mu_ref, w2_ref, w0_ref, a2_ref, a0_ref, g2_ref, kk_ref,
                     ka_ref, rk_ref, o_ref, bv_ref, *rest, rev, nct, n_tiles, dr):
    g_ref = rest[0] if len(rest) == 3 else None
    buf, s_ref = rest[-2:]
    step = pl.program_id(1)
    t = _tile_of_step(step, nct, n_tiles, rev)
    tm = TOKEN_TILE
    hw = GRID_W

    @pl.when(step == 0)
    def _():
        s_ref[...] = jnp.zeros_like(s_ref)

    is_ctx = t < nct
    first = jnp.logical_or(t == 0, t == nct)
    last = jnp.logical_or(t == nct - 1, t == n_tiles - 1)
    cur = cur_ref[0]
    buf[0:hw] = jnp.where(first, 0.0, prev_ref[0])
    buf[hw:hw + tm] = cur
    buf[hw + tm:hw + tm + hw] = jnp.where(last, 0.0, next_ref[0])
    cols = cur.shape[1]
    left = buf[hw - 1:hw - 1 + tm]
    right = buf[hw + 1:hw + 1 + tm]
    up = buf[0:tm]
    down = buf[2 * hw:2 * hw + tm]
    col = lax.broadcasted_iota(jnp.int32, (tm, 1), 0) & (hw - 1)
    lane = lax.broadcasted_iota(jnp.int32, (1, cols), 1)
    left_g = jnp.where(col == 0, 0.0, left)
    right_g = jnp.where(col == hw - 1, 0.0, right)
    l4 = lane & 3
    sh_lat = jnp.where(l4 == 0, left_g, jnp.where(l4 == 1, right_g, jnp.where(l4 == 2, up, down)))
    sh_ctx = jnp.where((lane & 1) == 0, left, right)
    sh = jnp.where(is_ctx, sh_ctx, sh_lat)
    m = cur + mu_ref[...] * (sh - cur)

    r = m[:, :dr]
    k = m[:, dr:2 * dr]
    v = m[:, 2 * dr:3 * dr]
    lw_w = w2_ref.shape[0]
    lw_a = a2_ref.shape[0]
    o3 = 3 * dr
    wd = m[:, o3:o3 + lw_w]
    ad = m[:, o3 + lw_w:o3 + lw_w + lw_a]
    gd = m[:, o3 + lw_w + lw_a:]
    lw = -_softplus(-(w0_ref[...] + _dot3(jnp.tanh(wd), w2_ref[...]))) - 0.5
    logw = -jnp.exp(lw)
    a_lr = jax.nn.sigmoid(a0_ref[...] + _dot3(ad, a2_ref[...]))
    ones_bd = _f(_block_diag_mask(dr, dr, RWKV_HEAD_DIM, RWKV_HEAD_DIM)).astype(BF16)
    kx = k * kk_ref[...]
    kk = kx * lax.rsqrt(_dot_sum(kx * kx, ones_bd) + 1e-12)
    k_d = k * (1.0 + (a_lr - 1.0) * ka_ref[...])
    b_d = kk * a_lr
    a_s = -kk
    bv_ref[0] = _dot_sum(r * k_d * rk_ref[...], ones_bd) * v
    if g_ref is not None:
        g_ref[0] = _dot3(jax.nn.sigmoid(gd), g2_ref[...])

    mk = _chunk_masks(rev)
    mk['eye_k'] = _f(lax.broadcasted_iota(jnp.int32, (dr, dr), 0) == lax.broadcasted_iota(jnp.int32, (dr, dr), 1))
    mk['bd2'] = _head_col_mask(HEADS * CHUNK, 2 * dr, CHUNK, RWKV_HEAD_DIM)
    nch = tm // CHUNK
    order = list(range(nch - 1, -1, -1) if rev else range(nch))
    outs = _rwkv_chunks(r, logw, k_d, v, a_s, b_d, s_ref, mk, order)
    for ci, o in zip(order, outs):
        o_ref[0, ci * CHUNK:(ci + 1) * CHUNK, :] = o


def _rwkv_dir(p, mu, w2p, w0, a2p, a0, g2, k_k, k_a, r_k, nct, rev):
    B, T, cols = p.shape
    tm = TOKEN_TILE
    n_tiles = T // tm
    dr = g2.shape[1]
    hpt = tm // GRID_W
    nhb = T // GRID_W
    tile = lambda s: _tile_of_step(s, nct, n_tiles, rev)
    full = lambda a: pl.BlockSpec(a.shape, lambda b, s: (0,) * a.ndim)
    tok = lambda c: pl.BlockSpec((1, tm, c), lambda b, s: (b, tile(s), 0))
    out_sds = jax.ShapeDtypeStruct((B, T, dr), F32)
    n_out = 2 if rev else 3
    return pl.pallas_call(
        functools.partial(_rwkv_dir_kernel, rev=rev, nct=nct, n_tiles=n_tiles, dr=dr),
        out_shape=[out_sds] * n_out,
        grid=(B, n_tiles),
        in_specs=[
            pl.BlockSpec((1, GRID_W, cols), lambda b, s: (b, jnp.maximum(tile(s) * hpt - 1, 0), 0)),
            tok(cols),
            pl.BlockSpec((1, GRID_W, cols), lambda b, s: (b, jnp.minimum((tile(s) + 1) * hpt, nhb - 1), 0)),
            full(mu), full(w2p), full(w0), full(a2p), full(a0), full(g2), full(k_k), full(k_a), full(r_k),
        ],
        out_specs=[tok(dr)] * n_out,
        scratch_shapes=[pltpu.VMEM((tm + 2 * GRID_W, cols), F32), pltpu.VMEM((dr, dr), F32)],
        compiler_params=_cparams(("parallel", "arbitrary")),
        name="rwkv_rev" if rev else "rwkv_fwd",
    )(p, p, p, mu, w2p, w0, a2p, a0, g2, k_k, k_a, r_k)


def _gdn_chunks(q, k, v, beta_k, g_i, g_k, s_ref, mk, order):
    n = len(order)
    kd = GDN_HEAD_DIM
    hk = HEADS * kd
    hc = HEADS * CHUNK
    rows = [slice(ci * CHUNK, (ci + 1) * CHUNK) for ci in order]
    each = lambda fn, *ls: [fn(*xs) for xs in zip(*ls)]
    pick = lambda x: [x[s] for s in rows]
    q, k, v, beta_k, g_i, g_k = pick(q), pick(k), pick(v), pick(beta_k), pick(g_i), pick(g_k)
    gi_all = _dot_xl(mk['tri'], jnp.concatenate(g_i, axis=1))
    gk_all = _dot_xl(mk['tri'], jnp.concatenate(g_k, axis=1))
    gc_i = [gi_all[:, i * hc:(i + 1) * hc] for i in range(n)]
    gc_k = [gk_all[:, i * hk:(i + 1) * hk] for i in range(n)]
    end = 0 if mk['rev'] else CHUNK - 1
    gl_k = [x[end:end + 1] for x in gc_k]
    incl = mk['incl_b']
    gc_j = each(lambda g: jnp.sum(g * mk['eye'], axis=0, keepdims=True), gc_i)
    decay = each(lambda gi, gj: jnp.where(incl, jnp.exp(jnp.where(incl, gi - gj, 0.0)), 0.0), gc_i, gc_j)
    kb = each(lambda k_, b_: k_ * b_, k, beta_k)
    ktT = each(lambda k_: _bdiag_t(k_, mk['bd_kt']), k)
    res = each(lambda kb_, q_, w_: _mmc(jnp.concatenate([kb_, q_], axis=0), w_), kb, q, ktT)
    m = each(lambda x, d_: x[:CHUNK] * d_ * mk['strict'], res, decay)
    attn = each(lambda x, d_: x[CHUNK:] * d_, res, decay)
    t = _tri_inv_each([-x for x in m], mk)
    egc = each(jnp.exp, gc_k)
    x = each(lambda v_, b_, kb_, e_: jnp.concatenate([v_ * b_, kb_ * e_], axis=1), v, beta_k, kb, egc)
    uw = each(lambda t_, x_: _mmc(t_, _bdiag(x_, mk['bd_x'])), t, x)
    auw = each(lambda a_, x_: _mmc(a_, _bdiag(x_, mk['bd_x'])), attn, uw)
    rh = each(lambda q_, e_, x_: q_ * e_ - x_[:, hk:], q, egc, auw)
    oloc = [x_[:, :hk] for x_ in auw]
    khT = each(lambda k_, gl, gc: jnp.concatenate([k_ * jnp.exp(gl - gc), jnp.zeros_like(k_)], axis=0).T,
               k, gl_k, gc_k)
    zero = jnp.zeros((CHUNK, 2 * kd), F32)
    heads = [slice(h * kd, (h + 1) * kd) for h in range(HEADS)]
    mn = [[_mmc(khT[i][sl], jnp.concatenate(
        [jnp.concatenate([uw[i][:, hk + h * kd:hk + (h + 1) * kd], uw[i][:, sl]], axis=1), zero], axis=0))
        for h, sl in enumerate(heads)] for i in range(n)]
    s = [s_ref[h] for h in range(HEADS)]
    outs = []
    for i in range(n):
        o_h = []
        for h, sl in enumerate(heads):
            o_h.append(_mmc(rh[i][:, sl], s[h]) + oloc[i][:, sl])
            mt = mk['eye_k'] * jnp.exp(gl_k[i][:, sl]) - mn[i][h][:, :kd]
            s[h] = _dot3(mt, s[h]) + mn[i][h][:, kd:]
        outs.append(jnp.concatenate(o_h, axis=1))
    for h in range(HEADS):
        s_ref[h] = s[h]
    return outs


def _gdn_dir_kernel(prev_ref, cur_ref, next_ref, cw_ref, selb_k, selg_i, selg_k, na_i, na_k, dt_i, dt_k,
                    o_ref, buf, s_ref, *, rev, nct, n_tiles, dg):
    step = pl.program_id(1)
    t = _tile_of_step(step, nct, n_tiles, rev)
    tm = TOKEN_TILE
    hl = GDN_CONV_HALO

    @pl.when(step == 0)
    def _():
        s_ref[...] = jnp.zeros_like(s_ref)

    first = jnp.logical_or(t == 0, t == nct)
    last = jnp.logical_or(t == nct - 1, t == n_tiles - 1)
    c3 = 3 * dg
    buf[0:hl] = jnp.where(first, 0.0, prev_ref[0])
    buf[hl:hl + tm] = cur_ref[0, :, :c3]
    buf[hl + tm:hl + tm + hl] = jnp.where(last, 0.0, next_ref[0])
    taps = cw_ref.shape[0]
    acc = None
    for j in range(taps):
        off = hl + j - taps // 2
        term = buf[off:off + tm] * cw_ref[j:j + 1, :]
        acc = term if acc is None else acc + term
    qkv = acc * jax.nn.sigmoid(acc)
    ones_bd = _f(_block_diag_mask(dg, dg, GDN_HEAD_DIM, GDN_HEAD_DIM)).astype(BF16)
    q = qkv[:, :dg]
    k = qkv[:, dg:2 * dg]
    v = qkv[:, 2 * dg:]
    q = q * lax.rsqrt(_dot_sum(q * q, ones_bd) + 1e-12) * (GDN_HEAD_DIM ** -0.5)
    k = k * lax.rsqrt(_dot_sum(k * k, ones_bd) + 1e-12)
    rest = cur_ref[0, :, 4 * dg:]
    beta_k = jax.nn.sigmoid(_dot_xr(rest, selb_k[...]))
    g_i = na_i[...] * _softplus(_dot_xr(rest, selg_i[...]) + dt_i[...])
    g_k = na_k[...] * _softplus(_dot_xr(rest, selg_k[...]) + dt_k[...])

    mk = _chunk_masks(rev)
    kd = GDN_HEAD_DIM
    mk['eye_k'] = _f(lax.broadcasted_iota(jnp.int32, (kd, kd), 0) == lax.broadcasted_iota(jnp.int32, (kd, kd), 1))
    mk['bd_kt'] = _f(_block_diag_mask(dg, HEADS * CHUNK, kd, CHUNK)).astype(BF16)
    mk['bd_x'] = _head_col_mask(HEADS * CHUNK, 2 * dg, CHUNK, kd)
    nch = tm // CHUNK
    order = list(range(nch - 1, -1, -1) if rev else range(nch))
    outs = _gdn_chunks(q, k, v, beta_k, g_i, g_k, s_ref, mk, order)
    for ci, o in zip(order, outs):
        o_ref[0, ci * CHUNK:(ci + 1) * CHUNK, :] = o


def _gdn_dir(p, conv_w, consts, nct, rev, dg):
    B, T, cols = p.shape
    tm = TOKEN_TILE
    n_tiles = T // tm
    hl = GDN_CONV_HALO
    hpt = tm // hl
    nhb = T // hl
    c3 = 3 * dg
    tile = lambda s: _tile_of_step(s, nct, n_tiles, rev)
    full = lambda a: pl.BlockSpec(a.shape, lambda b, s: (0,) * a.ndim)
    return pl.pallas_call(
        functools.partial(_gdn_dir_kernel, rev=rev, nct=nct, n_tiles=n_tiles, dg=dg),
        out_shape=jax.ShapeDtypeStruct((B, T, dg), F32),
        grid=(B, n_tiles),
        in_specs=[
            pl.BlockSpec((1, hl, c3), lambda b, s: (b, jnp.maximum(tile(s) * hpt - 1, 0), 0)),
            pl.BlockSpec((1, tm, cols), lambda b, s: (b, tile(s), 0)),
            pl.BlockSpec((1, hl, c3), lambda b, s: (b, jnp.minimum((tile(s) + 1) * hpt, nhb - 1), 0)),
            full(conv_w)] + [full(a) for a in consts],
        out_specs=pl.BlockSpec((1, tm, dg), lambda b, s: (b, tile(s), 0)),
        scratch_shapes=[pltpu.VMEM((tm + 2 * hl, c3), F32), pltpu.VMEM((HEADS, GDN_HEAD_DIM, GDN_HEAD_DIM), F32)],
        compiler_params=_cparams(("parallel", "arbitrary")),
        name="gdn_rev" if rev else "gdn_fwd",
    )(p, p, p, conv_w, *consts)


def _gdn_consts(a_log_d, dt_bias_d, d, dg):
    H = HEADS
    ncol = 4 * H
    def sel(base, width):
        m = np.zeros((ncol, H * width), np.float32)
        for h in range(H):
            m[base + d * H + h, h * width:(h + 1) * width] = 1.0
        return jnp.asarray(m, BF16)
    neg_a = -jnp.exp(a_log_d)
    exp_i = lambda x: jnp.repeat(x, CHUNK)[None, :]
    exp_k = lambda x: jnp.repeat(x, GDN_HEAD_DIM)[None, :]
    return (sel(0, GDN_HEAD_DIM), sel(2 * H, CHUNK), sel(2 * H, GDN_HEAD_DIM),
            exp_i(neg_a), exp_k(neg_a), exp_i(dt_bias_d), exp_k(dt_bias_d))


def _out_proj_kernel(h_ref, hy_ref, ro0_ref, ro1_ref, bv0_ref, bv1_ref, rg_ref, go0_ref, go1_ref, z_ref,
                     gnw_ref, gnb_ref, gdw_ref, w_ref, gate_ref, shift_ref, scale_ref, nw_ref, rwt_ref, rb_ref,
                     hn_ref, n_ref, lg_ref, *, splits):
    c0, c1, c2 = splits
    o = ro0_ref[0] + ro1_ref[0]
    ones_r = _f(_block_diag_mask(c1, c1, RWKV_HEAD_DIM, RWKV_HEAD_DIM)).astype(BF16)
    mean = _dot_sum(o, ones_r) * (1.0 / RWKV_HEAD_DIM)
    cen = o - mean
    var = _dot_sum(cen * cen, ones_r) * (1.0 / RWKV_HEAD_DIM)
    y_rw = (cen * lax.rsqrt(var + RWKV_GN_EPS) * gnw_ref[...] + gnb_ref[...] + bv0_ref[0] + bv1_ref[0]) * rg_ref[0]
    og = go0_ref[0] + go1_ref[0]
    ones_g = _f(_block_diag_mask(c2, c2, GDN_HEAD_DIM, GDN_HEAD_DIM)).astype(BF16)
    ms = _dot_sum(og * og, ones_g) * (1.0 / GDN_HEAD_DIM)
    z = z_ref[0]
    y_gd = og * lax.rsqrt(ms + NORM_EPS) * gdw_ref[...] * (z * jax.nn.sigmoid(z))

    acc = _dot(hy_ref[0].astype(BF16), w_ref[:c0, :])
    acc += _dot(y_rw.astype(BF16), w_ref[c0:c0 + c1, :])
    acc += _dot(y_gd.astype(BF16), w_ref[c0 + c1:c0 + c1 + c2, :])
    h = h_ref[0] + gate_ref[0, 0] * acc
    hn_ref[0] = h
    n = h * lax.rsqrt(jnp.mean(h * h, axis=-1, keepdims=True) + NORM_EPS) * nw_ref[...]
    n = n * (1.0 + scale_ref[0, 0]) + shift_ref[0, 0]
    n_ref[0] = n
    lg_ref[0] = _dot3(n, rwt_ref[...]) + rb_ref[...]


def _out_proj(h, y_hy, rw_parts, gd_parts, p_gd, gn_w, gn_b, gd_w, w_bf, gate, shift, scale, norm_w,
              router_w, router_b, n_ctx_tiles):
    B, T, D = h.shape
    tm = TOKEN_TILE
    E = router_w.shape[1]
    splits = (y_hy.shape[-1], rw_parts[0].shape[-1], gd_parts[0].shape[-1])
    c2 = splits[2]
    seg = lambda b, i: (b, (i >= n_ctx_tiles).astype(jnp.int32), 0, 0)
    tok = lambda c: pl.BlockSpec((1, tm, c), lambda b, i: (b, i, 0))
    full = lambda s: pl.BlockSpec(s, lambda b, i: (0,) * len(s))
    z_spec = pl.BlockSpec((1, tm, c2), lambda b, i: (b, i, 3))
    return pl.pallas_call(
        functools.partial(_out_proj_kernel, splits=splits),
        out_shape=[jax.ShapeDtypeStruct((B, T, D), F32), jax.ShapeDtypeStruct((B, T, D), F32),
                   jax.ShapeDtypeStruct((B, T, E), F32)],
        grid=(B, T // tm),
        in_specs=[tok(D), tok(splits[0])] + [tok(splits[1])] * 5 + [tok(c2), tok(c2), z_spec,
                  full((1, splits[1])), full((1, splits[1])), full((1, c2)), full(w_bf.shape),
                  pl.BlockSpec((1, 1, 1, D), seg), pl.BlockSpec((1, 1, 1, D), seg),
                  pl.BlockSpec((1, 1, 1, D), seg), full((1, D)), full(router_w.shape), full((1, E))],
        out_specs=[tok(D), tok(D), tok(E)],
        compiler_params=_cparams(("parallel", "parallel")),
        name="out_proj",
    )(h, y_hy, *rw_parts, *gd_parts, p_gd, gn_w, gn_b, gd_w, w_bf, gate, shift, scale, norm_w, router_w, router_b)


def _short_conv(u, w, b):
    L = u.shape[0]
    row = lax.broadcasted_iota(jnp.int32, (L, 1), 0)
    prev = jnp.where(row == 0, 0.0, pltpu.roll(u, 1, axis=0))
    nxt = jnp.where(row == L - 1, 0.0, pltpu.roll(u, L - 1, axis=0))
    return prev * w[0:1] + u * w[1:2] + nxt * w[2:3] + b


def _hyena_segment(u_ref, rows, sw_ref, sb_ref, skip_ref, cm_ref, sm_ref, p_ref, q_ref, pn_ref, d):
    conv = lambda g: _short_conv(u_ref[0, rows, g * d:(g + 1) * d], sw_ref[:, g * d:(g + 1) * d],
                                 sb_ref[:, g * d:(g + 1) * d])
    v = conv(2) * conv(1)
    vb = v.astype(BF16)
    a = _dot(cm_ref[...], vb)
    b = _dot(sm_ref[...], vb)
    p = p_ref[...]
    q = q_ref[...]
    yr = (a * p - b * q).astype(BF16)
    yi = (a * q + b * p).astype(BF16)
    y = _dot(cm_ref[...], yr) + _dot(sm_ref[...], yi)
    L = v.shape[0]
    sgn = jnp.where((lax.broadcasted_iota(jnp.int32, (L, 1), 0) & 1) == 0, 1.0, -1.0)
    a_nyq = jnp.sum(v * sgn, axis=0, keepdims=True)
    y = y + sgn * (a_nyq * pn_ref[...])
    return (y + v * skip_ref[...]) * conv(0)


def _hyena_kernel(u_ref, sw_ref, sb_ref, skip_ref, cm_l, sm_l, p_l, q_l, pn_l, cm_c, sm_c, p_c, q_c, pn_c,
                  y_ref, *, lc, with_ctx, d):
    T = u_ref.shape[1]
    lat = slice(lc, T)
    ctx = slice(0, lc)
    y_ref[0, lat, :] = _hyena_segment(u_ref, lat, sw_ref, sb_ref, skip_ref, cm_l, sm_l, p_l, q_l, pn_l, d)
    if with_ctx:
        y_ref[0, ctx, :] = _hyena_segment(u_ref, ctx, sw_ref, sb_ref, skip_ref, cm_c, sm_c, p_c, q_c, pn_c, d)
    else:
        y_ref[0, ctx, :] = jnp.zeros((lc, d), F32)


def _hyena(u, short_w, short_b, skip, lat_consts, ctx_consts, lc, with_ctx):
    B, T, cols = u.shape
    d = cols // 3
    one = pl.Buffered(1)
    full = lambda a: pl.BlockSpec(a.shape, lambda b: (0,) * a.ndim, pipeline_mode=one)
    consts = (short_w, short_b, skip) + tuple(lat_consts) + tuple(ctx_consts)
    return pl.pallas_call(
        functools.partial(_hyena_kernel, lc=lc, with_ctx=with_ctx, d=d),
        out_shape=jax.ShapeDtypeStruct((B, T, d), F32),
        grid=(B,),
        in_specs=[pl.BlockSpec((1, T, cols), lambda b: (b, 0, 0), pipeline_mode=one)] + [full(a) for a in consts],
        out_specs=pl.BlockSpec((1, T, d), lambda b: (b, 0, 0)),
        compiler_params=_cparams(("parallel",)),
        name="hyena",
    )(u, *consts)


def _moe_kernel(be_ref, nu_ref, x_ref, wgu_ref, bgu_ref, wdn_ref, bdn_ref, y_ref, wgu_bf, wdn_bf, *, d_expert):
    i = pl.program_id(0)
    prev = be_ref[jnp.maximum(i - 1, 0)]
    changed = jnp.logical_or(i == 0, be_ref[i] != prev)

    @pl.when(changed)
    def _():
        wgu_bf[...] = wgu_ref[0, 0].astype(BF16)
        wdn_bf[...] = wdn_ref[0, 0].astype(BF16)

    @pl.when(i < nu_ref[0])
    def _():
        gu = _dot(x_ref[...].astype(BF16), wgu_bf[...]) + bgu_ref[0, 0]
        gate = jnp.minimum(gu[:, :d_expert], SWIGLU_LIMIT)
        up = jnp.clip(gu[:, d_expert:], -SWIGLU_LIMIT, SWIGLU_LIMIT)
        glu = gate * jax.nn.sigmoid(gate * SWIGLU_ALPHA)
        act = ((up + 1.0) * glu).astype(BF16)
        y_ref[...] = _dot(act, wdn_bf[...]) + bdn_ref[0, 0]

    @pl.when(i >= nu_ref[0])
    def _():
        y_ref[...] = jnp.zeros_like(y_ref)


def _moe_experts(xs, block_e, n_used, w_gu, b_gu, w_dn, b_dn, layer):
    NP, D = xs.shape
    depth, E, _, F2 = w_gu.shape
    Fe = F2 // 2
    tm = MOE_TILE
    nb = NP // tm
    grid_spec = pltpu.PrefetchScalarGridSpec(
        num_scalar_prefetch=2,
        grid=(nb,),
        in_specs=[
            pl.BlockSpec((tm, D), lambda i, be, nu: (i, 0)),
            pl.BlockSpec((1, 1, D, F2), lambda i, be, nu: (layer, be[i], 0, 0)),
            pl.BlockSpec((1, 1, 1, F2), lambda i, be, nu: (layer, be[i], 0, 0)),
            pl.BlockSpec((1, 1, Fe, D), lambda i, be, nu: (layer, be[i], 0, 0)),
            pl.BlockSpec((1, 1, 1, D), lambda i, be, nu: (layer, be[i], 0, 0)),
        ],
        out_specs=pl.BlockSpec((tm, D), lambda i, be, nu: (i, 0)),
        scratch_shapes=[pltpu.VMEM((D, F2), BF16), pltpu.VMEM((Fe, D), BF16)],
    )
    return pl.pallas_call(
        functools.partial(_moe_kernel, d_expert=Fe),
        out_shape=jax.ShapeDtypeStruct((NP, D), F32),
        grid_spec=grid_spec,
        compiler_params=_cparams(("arbitrary",)),
        name="moe_experts",
    )(block_e, n_used, xs, w_gu, b_gu.reshape(depth, E, 1, F2), w_dn, b_dn.reshape(depth, E, 1, D))


def _combine_kernel(h_ref, yk_ref, rg_ref, g_ref, w_ref, o_ref, *, final):
    rg = rg_ref[0]
    y = yk_ref[0, 0].astype(F32) * rg[:, 0:1]
    for k in range(1, TOP_K):
        y += yk_ref[k, 0].astype(F32) * rg[:, k:k + 1]
    h = h_ref[0] + g_ref[0, 0] * y
    if final:
        h = h * lax.rsqrt(jnp.mean(h * h, axis=-1, keepdims=True) + NORM_EPS) * w_ref[...]
    o_ref[0] = h


def _combine(h, yk, route_gates, gate, w, n_ctx_tiles, final):
    B, T, D = h.shape
    tm = TOKEN_TILE
    tok = lambda c: pl.BlockSpec((1, tm, c), lambda b, i: (b, i, 0))
    yk_spec = pl.BlockSpec((TOP_K, 1, tm, D), lambda b, i: (0, b, i, 0))
    nseg = gate.shape[1]
    seg = lambda b, i: (b, jnp.minimum((i >= n_ctx_tiles).astype(jnp.int32), nseg - 1), 0, 0)
    return pl.pallas_call(
        functools.partial(_combine_kernel, final=final),
        out_shape=jax.ShapeDtypeStruct((B, T, D), F32),
        grid=(B, T // tm),
        in_specs=[tok(D), yk_spec, tok(TOP_K), pl.BlockSpec((1, 1, 1, D), seg),
                  pl.BlockSpec((1, D), lambda b, i: (0, 0))],
        out_specs=tok(D),
        compiler_params=_cparams(("parallel", "parallel")),
        name="combine",
    )(h, yk, route_gates, gate, w)


def _dft_mats(L):
    n2 = 2 * L
    f = lax.broadcasted_iota(jnp.int32, (L, L), 0)
    t = lax.broadcasted_iota(jnp.int32, (L, L), 1)
    ang = ((f * t) % n2).astype(F32) * (2.0 * math.pi / n2)
    return jnp.cos(ang), jnp.sin(ang)


def _hyena_filter(L, w1, b1, w2, b2, w3, b3, w4, freq, d_hy):
    pos = jnp.arange(L, dtype=F32)
    t = jnp.linspace(0.0, 1.0, L, dtype=F32)[:, None]
    bands = jnp.linspace(1e-4, HYENA_BANDS - 1, HYENA_BANDS, dtype=F32)
    ang = (2.0 * math.pi / L) * pos[:, None] * bands
    z = jnp.concatenate([t, jnp.cos(ang), -jnp.sin(ang)], axis=-1)
    hdn = jnp.sin(freq * (jnp.dot(z, w1, precision=_HI) + b1))
    hdn = jnp.sin(freq * (jnp.dot(hdn, w2, precision=_HI) + b2))
    hdn = jnp.sin(freq * (jnp.dot(hdn, w3, precision=_HI) + b3))
    h = jnp.dot(hdn, w4, precision=_HI)
    deltas = jnp.linspace(math.log(HYENA_DECAY_TARGET) / HYENA_SLOW_PCT,
                          math.log(HYENA_DECAY_TARGET) / HYENA_FAST_PCT, d_hy, dtype=F32)
    window = jnp.exp(-t * jnp.abs(deltas))
    h_fwd = h[:, :d_hy] * window
    h_bwd = h[:, d_hy:] * window
    k_lo = h_fwd
    k_hi = jnp.concatenate([jnp.zeros((1, d_hy), F32), h_bwd[:0:-1]], axis=0)
    return k_lo, k_hi


def _hyena_consts(dft, fparams, d_hy):
    cm, sm = dft
    L = cm.shape[0]
    k_lo, k_hi = _hyena_filter(L, *fparams, d_hy)
    sgn = jnp.where(jnp.arange(L) % 2 == 0, 1.0, -1.0).astype(F32)[:, None]
    wf = jnp.where(jnp.arange(L) == 0, 1.0, 2.0).astype(F32)[:, None] / (2 * L)
    three_pass = lax.Precision.HIGH
    p = (jnp.dot(cm, k_lo, precision=three_pass) + sgn * jnp.dot(cm, k_hi, precision=three_pass)) * wf
    q = (jnp.dot(sm, k_lo, precision=three_pass) + sgn * jnp.dot(sm, k_hi, precision=three_pass)) * wf
    pn = jnp.sum(sgn * (k_lo + k_hi), axis=0, keepdims=True) / (2 * L)
    return cm.astype(BF16), sm.astype(BF16), p, q, pn


def _pad_rows(w, lo, total):
    return jnp.zeros((total, w.shape[1]), w.dtype).at[lo:lo + w.shape[0]].set(w)


def _rwkv_mixer_parts(p, mu, w0, w2, a0, a2, g2, k_k, k_a, r_k, nct):
    row = lambda x: x[None, :]
    lw_w, lw_a = w2.shape[1], a2.shape[1]
    outs = []
    for d in range(2):
        w2p = _pad_rows(w2[d], d * lw_w, 2 * lw_w)
        a2p = _pad_rows(a2[d], d * lw_a, 2 * lw_a)
        outs.append(_rwkv_dir(p, row(mu), w2p, row(w0[d]), a2p, row(a0[d]), g2, row(k_k), row(k_a), row(r_k),
                              nct, rev=bool(d)))
    (o0, bv0, g), (o1, bv1) = outs
    return o0, o1, bv0, bv1, g


def _gdn_mixer_parts(p, conv_w, a_log, dt_bias, nct, dg):
    return [_gdn_dir(p, conv_w, _gdn_consts(a_log[d], dt_bias[d], d, dg), nct, rev=bool(d), dg=dg) for d in range(2)]


def _moe(n_bf, logits, w_gu, b_gu, w_dn, b_dn, layer):
    Nt, D = n_bf.shape
    E = w_gu.shape[1]
    tm = MOE_TILE
    top_logit, top_idx = lax.top_k(logits, TOP_K)
    gates = jax.nn.softmax(top_logit, axis=-1)
    A = Nt * TOP_K
    flat_e = top_idx.T.reshape(A)
    experts = jnp.arange(E, dtype=jnp.int32)
    counts = jnp.sum((flat_e[:, None] == experts[None, :]).astype(jnp.int32), axis=0)
    blocks_per = (counts + tm - 1) // tm
    block_end = jnp.cumsum(blocks_per)
    n_blocks = -(-A // tm) + E
    n_slots = n_blocks * tm
    block_e = jnp.minimum(jnp.sum((block_end[None, :] <= jnp.arange(n_blocks)[:, None]).astype(jnp.int32), axis=1),
                          E - 1).astype(jnp.int32)
    n_used = block_end[-1:].astype(jnp.int32)
    need = blocks_per * tm - counts
    pad_keys = jnp.where(jnp.arange(tm - 1)[None, :] < need[:, None], experts[:, None], E).reshape(E * (tm - 1))
    tail = jnp.full((n_slots - A - E * (tm - 1),), E, jnp.int32)
    keys = jnp.concatenate([flat_e, pad_keys, tail])
    ids = jnp.arange(n_slots, dtype=jnp.int32)
    _, slot_id = lax.sort((keys, ids), num_keys=1)
    _, slot_of_id = lax.sort((slot_id, ids), num_keys=1)
    dest = slot_of_id[:A]
    slot_tok = jnp.where(slot_id < A, slot_id % Nt, 0)
    xs = n_bf[slot_tok]
    yb = _moe_experts(xs, block_e, n_used, w_gu, b_gu, w_dn, b_dn, layer)
    return yb[dest.reshape(TOP_K, Nt)], gates


def kernel(x, c, ctx, c_ctx, ada_w, ada_b, norm_mix_w, norm_ffn_w, final_norm_w, w_in, w_out, hy_short_w, hy_short_b, hy_f_w1, hy_f_b1, hy_f_w2, hy_f_b2, hy_f_w3, hy_f_b3, hy_f_w4, hy_f_freq, hy_skip, rw_mu, rw_w0, rw_w2, rw_a0, rw_a2, rw_g2, rw_k_k, rw_k_a, rw_r_k, rw_gn_w, rw_gn_b, gdn_conv_w, gdn_a_log, gdn_dt_bias, gdn_norm_w, moe_router_w, moe_router_b, moe_w_gu, moe_b_gu, moe_w_dn, moe_b_dn):
    B, L, D = x.shape
    Lc = ctx.shape[1]
    depth = ada_w.shape[0]
    d_hy = hy_skip.shape[-1]
    hy_cols = 3 * d_hy
    rw_cols = rw_mu.shape[-1]
    gd_cols = w_in.shape[-1] - hy_cols - rw_cols
    dg = gdn_conv_w.shape[-1] // 3
    assert gdn_a_log.shape[-1] == HEADS and rw_w0.shape[-1] == HEADS * RWKV_HEAD_DIM and dg == HEADS * GDN_HEAD_DIM
    assert Lc % TOKEN_TILE == 0 and L % TOKEN_TILE == 0 and TOKEN_TILE % GRID_W == 0
    splits = (hy_cols, rw_cols, gd_cols)
    nct = Lc // TOKEN_TILE
    row = lambda v: v[None, :]

    h = jnp.concatenate([ctx, x], axis=1)
    s_lat = jax.nn.silu(c)
    s_ctx = jax.nn.silu(c_ctx)
    dft_lat = _dft_mats(L)
    dft_ctx = _dft_mats(Lc)
    mods = None
    for l in range(depth):
        last = l == depth - 1
        ml = jnp.dot(s_lat, ada_w[l], precision=_HI) + ada_b[l]
        mc = jnp.dot(s_ctx, ada_w[l], precision=_HI) + ada_b[l]
        mods = jnp.stack([jnp.broadcast_to(mc, ml.shape), ml], axis=1).reshape(B, 2, N_MOD, 1, D)
        mod = lambda j: mods[:, :, j]
        p_hy, p_rw, p_gd = _in_proj(h, mod(0), mod(1), row(norm_mix_w[l]), w_in[l].astype(BF16), splits, nct)

        fparams = (hy_f_w1[l], hy_f_b1[l], hy_f_w2[l], hy_f_b2[l], hy_f_w3[l], hy_f_b3[l], hy_f_w4[l], hy_f_freq[l])
        y_hy = _hyena(p_hy, hy_short_w[l], row(hy_short_b[l]), row(hy_skip[l]), _hyena_consts(dft_lat, fparams, d_hy),
                      _hyena_consts(dft_ctx, fparams, d_hy), Lc, with_ctx=not last)
        rw_parts = _rwkv_mixer_parts(p_rw, rw_mu[l], rw_w0[l], rw_w2[l], rw_a0[l], rw_a2[l], rw_g2[l],
                                     rw_k_k[l], rw_k_a[l], rw_r_k[l], nct)
        gd_parts = _gdn_mixer_parts(p_gd, gdn_conv_w[l], gdn_a_log[l], gdn_dt_bias[l], nct, dg)

        h, n_bf, logits = _out_proj(h, y_hy, rw_parts, gd_parts, p_gd, row(rw_gn_w[l]), row(rw_gn_b[l]),
                                    row(jnp.tile(gdn_norm_w[l], HEADS)), w_out[l].astype(BF16), mod(2), mod(3),
                                    mod(4), row(norm_ffn_w[l]), moe_router_w[l], row(moe_router_b[l]), nct)
        if last:
            h, n_bf, logits = h[:, Lc:], n_bf[:, Lc:], logits[:, Lc:]
        Tm = h.shape[1]
        yk, route_gates = _moe(n_bf.reshape(B * Tm, D), logits.reshape(B * Tm, -1),
                               moe_w_gu, moe_b_gu, moe_w_dn, moe_b_dn, l)
        h = _combine(h, yk.reshape(TOP_K, B, Tm, D), route_gates.reshape(B, Tm, TOP_K),
                     mods[:, 1:2, 5] if last else mod(5), row(final_norm_w), 0 if last else nct, final=last)
    return h
```

```python
import functools
import math

import jax
import jax.numpy as jnp
import numpy as np
from jax import lax
from jax.experimental import pallas as pl
from jax.experimental.pallas import tpu as pltpu

F32 = jnp.float32
BF16 = jnp.bfloat16

NORM_EPS = 1e-6
N_MOD = 6
GRID_W = 64
CHUNK = 64
HEADS = 4
HYENA_BANDS = 16
HYENA_DECAY_TARGET = 1e-2
HYENA_FAST_PCT = 0.3
HYENA_SLOW_PCT = 1.5
RWKV_HEAD_DIM = 64
RWKV_GN_EPS = 64e-5
GDN_HEAD_DIM = 128
GDN_CONV_HALO = 8
TOP_K = 4
SWIGLU_ALPHA = 1.702
SWIGLU_LIMIT = 7.0

TOKEN_TILE = 256
MOE_TILE = 512
VMEM_LIMIT = 56 * 1024 * 1024
_HI = lax.Precision.HIGHEST


def _cparams(sem):
    return pltpu.CompilerParams(dimension_semantics=sem, vmem_limit_bytes=VMEM_LIMIT)


def _dot(a, b):
    return jnp.dot(a, b, preferred_element_type=F32)


def _split2(x):
    hi = x.astype(BF16)
    lo = (x - hi.astype(F32)).astype(BF16)
    return hi, lo


def _split3(x):
    hi = x.astype(BF16)
    r1 = x - hi.astype(F32)
    mid = r1.astype(BF16)
    lo = (r1 - mid.astype(F32)).astype(BF16)
    return hi, mid, lo


def _dot3(a, b):
    ah, al = _split2(a)
    bh, bl = _split2(b)
    return _dot(ah, bh) + _dot(al, bh) + _dot(ah, bl)


def _dot_xl(sel_bf, x):
    hi, mid, lo = _split3(x)
    return _dot(sel_bf, hi) + _dot(sel_bf, mid) + _dot(sel_bf, lo)


def _dot_xr(x, sel_bf):
    hi, mid, lo = _split3(x)
    return _dot(hi, sel_bf) + _dot(mid, sel_bf) + _dot(lo, sel_bf)


def _dot_sum(x, ones_bf):
    hi, lo = _split2(x)
    return _dot(hi, ones_bf) + _dot(lo, ones_bf)


def _f(mask):
    return jnp.where(mask, 1.0, 0.0).astype(F32)


def _block_diag_mask(rows, cols, rblk, cblk):
    r = lax.broadcasted_iota(jnp.int32, (rows, cols), 0) >> int(math.log2(rblk))
    c = lax.broadcasted_iota(jnp.int32, (rows, cols), 1) >> int(math.log2(cblk))
    return r == c


def _softplus(x):
    return jnp.maximum(x, 0.0) + jnp.log1p(jnp.exp(-jnp.abs(x)))


def _chunk_masks(rev):
    shape = (CHUNK, HEADS * CHUNK)
    i = lax.broadcasted_iota(jnp.int32, shape, 0)
    j = lax.broadcasted_iota(jnp.int32, shape, 1) & (CHUNK - 1)
    strict = (j > i) if rev else (j < i)
    incl = (j >= i) if rev else (j <= i)
    same16 = (i >> 4) == (j >> 4)
    same32 = (i >> 5) == (j >> 5)
    ii = lax.broadcasted_iota(jnp.int32, (CHUNK, CHUNK), 0)
    jj = lax.broadcasted_iota(jnp.int32, (CHUNK, CHUNK), 1)
    tri = _f((jj >= ii) if rev else (jj <= ii)).astype(BF16)
    bdf = _f(_block_diag_mask(HEADS * CHUNK, HEADS * CHUNK, CHUNK, CHUNK))
    return dict(rev=rev, strict=_f(strict), incl=_f(incl), incl_b=incl, eye=_f(i == j), m16=_f(same16),
                m32=_f(jnp.logical_and(same32, jnp.logical_not(same16))), m64=_f(jnp.logical_not(same32)),
                tri=tri, bdf=bdf, bd=bdf.astype(BF16))


def _head_col_mask(rows, cols, rblk, cblk):
    r = lax.broadcasted_iota(jnp.int32, (rows, cols), 0) >> int(math.log2(rblk))
    c = (lax.broadcasted_iota(jnp.int32, (rows, cols), 1) >> int(math.log2(cblk))) & (HEADS - 1)
    return _f(r == c).astype(BF16)


def _mmc(a, b):
    return _dot(a.astype(BF16), b.astype(BF16))


def _bdiag(y, bd):
    return jnp.concatenate([y.astype(BF16)] * HEADS, axis=0) * bd


def _bdiag_t(y, bd):
    return jnp.concatenate([y] * HEADS, axis=0).T.astype(BF16) * bd


def _tri_inv_each(a_list, mk):
    bd = mk['bd']
    mmh = lambda xs, ys: [_mmc(x, _bdiag(y, bd)) for x, y in zip(xs, ys)]
    d = [a * mk['m16'] for a in a_list]
    e = [a * mk['m32'] for a in a_list]
    f = [a * mk['m64'] for a in a_list]
    d2 = mmh(d, d)
    d4 = mmh(d2, d2)
    d8 = mmh(d4, d4)
    t = [mk['eye'] + x for x in d]
    for p in (d2, d4, d8):
        t = [x + y for x, y in zip(t, mmh(t, p))]
    for p in (e, f):
        t = [x + y for x, y in zip(t, mmh(mmh(t, p), t))]
    return t


def _tile_of_step(s, nct, n_tiles, rev):
    if not rev:
        return s
    return jnp.where(s < nct, nct - 1 - s, n_tiles - 1 - (s - nct))


def _in_proj_kernel(h_ref, shift_ref, scale_ref, nw_ref, w_ref, hy_ref, rw_ref, gd_ref, *, splits):
    h = h_ref[0]
    n = h * lax.rsqrt(jnp.mean(h * h, axis=-1, keepdims=True) + NORM_EPS) * nw_ref[...]
    n = (n * (1.0 + scale_ref[0, 0]) + shift_ref[0, 0]).astype(BF16)
    c0, c1, c2 = splits
    hy_ref[0] = _dot(n, w_ref[:, :c0])
    rw_ref[0] = _dot(n, w_ref[:, c0:c0 + c1])
    gd_ref[0] = _dot(n, w_ref[:, c0 + c1:c0 + c1 + c2])


def _in_proj(h, shift, scale, norm_w, w_bf, splits, n_ctx_tiles):
    B, T, D = h.shape
    tm = TOKEN_TILE
    seg = lambda b, i: (b, (i >= n_ctx_tiles).astype(jnp.int32), 0, 0)
    cols = w_bf.shape[1]
    outs = [jax.ShapeDtypeStruct((B, T, c), F32) for c in splits]
    return pl.pallas_call(
        functools.partial(_in_proj_kernel, splits=splits),
        out_shape=outs,
        grid=(B, T // tm),
        in_specs=[
            pl.BlockSpec((1, tm, D), lambda b, i: (b, i, 0)),
            pl.BlockSpec((1, 1, 1, D), seg),
            pl.BlockSpec((1, 1, 1, D), seg),
            pl.BlockSpec((1, D), lambda b, i: (0, 0)),
            pl.BlockSpec((D, cols), lambda b, i: (0, 0)),
        ],
        out_specs=[pl.BlockSpec((1, tm, c), lambda b, i: (b, i, 0)) for c in splits],
        compiler_params=_cparams(("parallel", "parallel")),
        name="in_proj",
    )(h, shift, scale, norm_w, w_bf)


def _rwkv_chunks(r, lw, k, v, a, b, s_ref, mk, order):
    n = len(order)
    hk = r.shape[1]
    rows = [slice(ci * CHUNK, (ci + 1) * CHUNK) for ci in order]
    each = lambda fn, *ls: [fn(*xs) for xs in zip(*ls)]
    pick = lambda x: [x[s] for s in rows]
    r, lw, k, v, a, b = pick(r), pick(lw), pick(k), pick(v), pick(a), pick(b)
    bd, bd2 = mk['bd'], mk['bd2']
    c_all = _dot_xl(mk['tri'], jnp.concatenate(lw, axis=1))
    c = [c_all[:, i * hk:(i + 1) * hk] for i in range(n)]
    end = 0 if mk['rev'] else CHUNK - 1
    c_end = [x[end:end + 1] for x in c]
    rt = each(lambda r_, c_: r_ * jnp.exp(c_), r, c)
    at = each(lambda a_, c_, lw_: a_ * jnp.exp(c_ - lw_), a, c, lw)
    enc = each(lambda c_: jnp.exp(-c_), c)
    bt = each(lambda b_, e_: b_ * e_, b, enc)
    kt = each(lambda k_, e_: k_ * e_, k, enc)
    e2 = each(lambda c_, ce: jnp.exp(ce - c_), c, c_end)
    bb = each(lambda b_, e_: b_ * e_, b, e2)
    kb = each(lambda k_, e_: k_ * e_, k, e2)
    btk = each(lambda bt_, kt_: jnp.concatenate(
        [_bdiag_t(bt_, bd), _bdiag_t(kt_, bd)], axis=1), bt, kt)
    res = each(lambda at_, rt_, w_: _mmc(jnp.concatenate([at_, rt_], axis=0), w_), at, rt, btk)
    hc = HEADS * CHUNK
    a_ab = [x[:CHUNK, :hc] * mk['strict'] for x in res]
    a_ak = [x[:CHUNK, hc:] * mk['strict'] for x in res]
    a_rb = [x[CHUNK:, :hc] * mk['incl'] for x in res]
    a_rk = [x[CHUNK:, hc:] * mk['incl'] for x in res]
    t = _tri_inv_each(a_ab, mk)
    vbd = each(lambda v_: _bdiag(v_, bd), v)
    akv = each(_mmc, a_ak, vbd)
    au = each(lambda t_, at_, akv_: _mmc(t_, _bdiag(jnp.concatenate([at_, akv_], axis=1), bd2)), t, at, akv)
    aubd = each(lambda x: _bdiag(x, bd2), au)
    ru = each(_mmc, a_rb, aubd)
    rh = each(lambda rt_, x: rt_ + x[:, :hk], rt, ru)
    oloc = each(lambda a_, vb, x: _mmc(a_, vb) + x[:, hk:], a_rk, vbd, ru)
    mn = each(lambda bb_, kb_, au_, v_: _mmc(
        jnp.concatenate([bb_, kb_], axis=0).T,
        jnp.concatenate([au_, jnp.concatenate([jnp.zeros_like(v_), v_], axis=1)], axis=0)), bb, kb, au, v)
    mt = each(lambda x, ce: x[:, :hk] * mk['bdf'] + mk['eye_k'] * jnp.exp(ce), mn, c_end)
    nt = [x[:, hk:] * mk['bdf'] for x in mn]
    outs = []
    s = s_ref[...]
    for i in range(n):
        outs.append(_mmc(rh[i], s) + oloc[i])
        s = _dot3(mt[i], s) + nt[i]
    s_ref[...] = s
    return outs


def _rwkv_dir_kernel(prev_ref, cur_ref, next_ref, mu_ref, w2_ref, w0_ref, a2_ref, a0_ref, g2_ref, kk_ref,
                     ka_ref, rk_ref, o_ref, bv_ref, *rest, rev, nct, n_tiles, dr):
    g_ref = rest[0] if len(rest) == 3 else None
    buf, s_ref = rest[-2:]
    step = pl.program_id(1)
    t = _tile_of_step(step, nct, n_tiles, rev)
    tm = TOKEN_TILE
    hw = GRID_W

    @pl.when(step == 0)
    def _():
        s_ref[...] = jnp.zeros_like(s_ref)

    is_ctx = t < nct
    first = jnp.logical_or(t == 0, t == nct)
    last = jnp.logical_or(t == nct - 1, t == n_tiles - 1)
    cur = cur_ref[0]
    buf[0:hw] = jnp.where(first, 0.0, prev_ref[0])
    buf[hw:hw + tm] = cur
    buf[hw + tm:hw + tm + hw] = jnp.where(last, 0.0, next_ref[0])
    cols = cur.shape[1]
    left = buf[hw - 1:hw - 1 + tm]
    right = buf[hw + 1:hw + 1 + tm]
    up = buf[0:tm]
    down = buf[2 * hw:2 * hw + tm]
    col = lax.broadcasted_iota(jnp.int32, (tm, 1), 0) & (hw - 1)
    lane = lax.broadcasted_iota(jnp.int32, (1, cols), 1)
    left_g = jnp.where(col == 0, 0.0, left)
    right_g = jnp.where(col == hw - 1, 0.0, right)
    l4 = lane & 3
    sh_lat = jnp.where(l4 == 0, left_g, jnp.where(l4 == 1, right_g, jnp.where(l4 == 2, up, down)))
    sh_ctx = jnp.where((lane & 1) == 0, left, right)
    sh = jnp.where(is_ctx, sh_ctx, sh_lat)
    m = cur + mu_ref[...] * (sh - cur)

    r = m[:, :dr]
    k = m[:, dr:2 * dr]
    v = m[:, 2 * dr:3 * dr]
    lw_w = w2_ref.shape[0]
    lw_a = a2_ref.shape[0]
    o3 = 3 * dr
    wd = m[:, o3:o3 + lw_w]
    ad = m[:, o3 + lw_w:o3 + lw_w + lw_a]
    gd = m[:, o3 + lw_w + lw_a:]
    lw = -_softplus(-(w0_ref[...] + _dot3(jnp.tanh(wd), w2_ref[...]))) - 0.5
    logw = -jnp.exp(lw)
    a_lr = jax.nn.sigmoid(a0_ref[...] + _dot3(ad, a2_ref[...]))
    ones_bd = _f(_block_diag_mask(dr, dr, RWKV_HEAD_DIM, RWKV_HEAD_DIM)).astype(BF16)
    kx = k * kk_ref[...]
    kk = kx * lax.rsqrt(_dot_sum(kx * kx, ones_bd) + 1e-12)
    k_d = k * (1.0 + (a_lr - 1.0) * ka_ref[...])
    b_d = kk * a_lr
    a_s = -kk
    bv_ref[0] = _dot_sum(r * k_d * rk_ref[...], ones_bd) * v
    if g_ref is not None:
        g_ref[0] = _dot3(jax.nn.sigmoid(gd), g2_ref[...])

    mk = _chunk_masks(rev)
    mk['eye_k'] = _f(lax.broadcasted_iota(jnp.int32, (dr, dr), 0) == lax.broadcasted_iota(jnp.int32, (dr, dr), 1))
    mk['bd2'] = _head_col_mask(HEADS * CHUNK, 2 * dr, CHUNK, RWKV_HEAD_DIM)
    nch = tm // CHUNK
    order = list(range(nch - 1, -1, -1) if rev else range(nch))
    outs = _rwkv_chunks(r, logw, k_d, v, a_s, b_d, s_ref, mk, order)
    for ci, o in zip(order, outs):
        o_ref[0, ci * CHUNK:(ci + 1) * CHUNK, :] = o


def _rwkv_dir(p, mu, w2p, w0, a2p, a0, g2, k_k, k_a, r_k, nct, rev):
    B, T, cols = p.shape
    tm = TOKEN_TILE
    n_tiles = T // tm
    dr = g2.shape[1]
    hpt = tm // GRID_W
    nhb = T // GRID_W
    tile = lambda s: _tile_of_step(s, nct, n_tiles, rev)
    full = lambda a: pl.BlockSpec(a.shape, lambda b, s: (0,) * a.ndim)
    tok = lambda c: pl.BlockSpec((1, tm, c), lambda b, s: (b, tile(s), 0))
    out_sds = jax.ShapeDtypeStruct((B, T, dr), F32)
    n_out = 2 if rev else 3
    return pl.pallas_call(
        functools.partial(_rwkv_dir_kernel, rev=rev, nct=nct, n_tiles=n_tiles, dr=dr),
        out_shape=[out_sds] * n_out,
        grid=(B, n_tiles),
        in_specs=[
            pl.BlockSpec((1, GRID_W, cols), lambda b, s: (b, jnp.maximum(tile(s) * hpt - 1, 0), 0)),
            tok(cols),
            pl.BlockSpec((1, GRID_W, cols), lambda b, s: (b, jnp.minimum((tile(s) + 1) * hpt, nhb - 1), 0)),
            full(mu), full(w2p), full(w0), full(a2p), full(a0), full(g2), full(k_k), full(k_a), full(r_k),
        ],
        out_specs=[tok(dr)] * n_out,
        scratch_shapes=[pltpu.VMEM((tm + 2 * GRID_W, cols), F32), pltpu.VMEM((dr, dr), F32)],
        compiler_params=_cparams(("parallel", "arbitrary")),
        name="rwkv_rev" if rev else "rwkv_fwd",
    )(p, p, p, mu, w2p, w0, a2p, a0, g2, k_k, k_a, r_k)


def _gdn_chunks(q, k, v, beta_k, g_i, g_k, s_ref, mk, order):
    n = len(order)
    kd = GDN_HEAD_DIM
    hk = HEADS * kd
    hc = HEADS * CHUNK
    rows = [slice(ci * CHUNK, (ci + 1) * CHUNK) for ci in order]
    each = lambda fn, *ls: [fn(*xs) for xs in zip(*ls)]
    pick = lambda x: [x[s] for s in rows]
    q, k, v, beta_k, g_i, g_k = pick(q), pick(k), pick(v), pick(beta_k), pick(g_i), pick(g_k)
    gi_all = _dot_xl(mk['tri'], jnp.concatenate(g_i, axis=1))
    gk_all = _dot_xl(mk['tri'], jnp.concatenate(g_k, axis=1))
    gc_i = [gi_all[:, i * hc:(i + 1) * hc] for i in range(n)]
    gc_k = [gk_all[:, i * hk:(i + 1) * hk] for i in range(n)]
    end = 0 if mk['rev'] else CHUNK - 1
    gl_k = [x[end:end + 1] for x in gc_k]
    incl = mk['incl_b']
    gc_j = each(lambda g: jnp.sum(g * mk['eye'], axis=0, keepdims=True), gc_i)
    decay = each(lambda gi, gj: jnp.where(incl, jnp.exp(jnp.where(incl, gi - gj, 0.0)), 0.0), gc_i, gc_j)
    kb = each(lambda k_, b_: k_ * b_, k, beta_k)
    ktT = each(lambda k_: _bdiag_t(k_, mk['bd_kt']), k)
    res = each(lambda kb_, q_, w_: _mmc(jnp.concatenate([kb_, q_], axis=0), w_), kb, q, ktT)
    m = each(lambda x, d_: x[:CHUNK] * d_ * mk['strict'], res, decay)
    attn = each(lambda x, d_: x[CHUNK:] * d_, res, decay)
    t = _tri_inv_each([-x for x in m], mk)
    egc = each(jnp.exp, gc_k)
    x = each(lambda v_, b_, kb_, e_: jnp.concatenate([v_ * b_, kb_ * e_], axis=1), v, beta_k, kb, egc)
    uw = each(lambda t_, x_: _mmc(t_, _bdiag(x_, mk['bd_x'])), t, x)
    auw = each(lambda a_, x_: _mmc(a_, _bdiag(x_, mk['bd_x'])), attn, uw)
    rh = each(lambda q_, e_, x_: q_ * e_ - x_[:, hk:], q, egc, auw)
    oloc = [x_[:, :hk] for x_ in auw]
    khT = each(lambda k_, gl, gc: jnp.concatenate([k_ * jnp.exp(gl - gc), jnp.zeros_like(k_)], axis=0).T,
               k, gl_k, gc_k)
    zero = jnp.zeros((CHUNK, 2 * kd), F32)
    heads = [slice(h * kd, (h + 1) * kd) for h in range(HEADS)]
    mn = [[_mmc(khT[i][sl], jnp.concatenate(
        [jnp.concatenate([uw[i][:, hk + h * kd:hk + (h + 1) * kd], uw[i][:, sl]], axis=1), zero], axis=0))
        for h, sl in enumerate(heads)] for i in range(n)]
    s = [s_ref[h] for h in range(HEADS)]
    outs = []
    for i in range(n):
        o_h = []
        for h, sl in enumerate(heads):
            o_h.append(_mmc(rh[i][:, sl], s[h]) + oloc[i][:, sl])
            mt = mk['eye_k'] * jnp.exp(gl_k[i][:, sl]) - mn[i][h][:, :kd]
            s[h] = _dot3(mt, s[h]) + mn[i][h][:, kd:]
        outs.append(jnp.concatenate(o_h, axis=1))
    for h in range(HEADS):
        s_ref[h] = s[h]
    return outs


def _gdn_dir_kernel(prev_ref, cur_ref, next_ref, cw_ref, selb_k, selg_i, selg_k, na_i, na_k, dt_i, dt_k,
                    o_ref, buf, s_ref, *, rev, nct, n_tiles, dg):
    step = pl.program_id(1)
    t = _tile_of_step(step, nct, n_tiles, rev)
    tm = TOKEN_TILE
    hl = GDN_CONV_HALO

    @pl.when(step == 0)
    def _():
        s_ref[...] = jnp.zeros_like(s_ref)

    first = jnp.logical_or(t == 0, t == nct)
    last = jnp.logical_or(t == nct - 1, t == n_tiles - 1)
    c3 = 3 * dg
    buf[0:hl] = jnp.where(first, 0.0, prev_ref[0])
    buf[hl:hl + tm] = cur_ref[0, :, :c3]
    buf[hl + tm:hl + tm + hl] = jnp.where(last, 0.0, next_ref[0])
    taps = cw_ref.shape[0]
    acc = None
    for j in range(taps):
        off = hl + j - taps // 2
        term = buf[off:off + tm] * cw_ref[j:j + 1, :]
        acc = term if acc is None else acc + term
    qkv = acc * jax.nn.sigmoid(acc)
    ones_bd = _f(_block_diag_mask(dg, dg, GDN_HEAD_DIM, GDN_HEAD_DIM)).astype(BF16)
    q = qkv[:, :dg]
    k = qkv[:, dg:2 * dg]
    v = qkv[:, 2 * dg:]
    q = q * lax.rsqrt(_dot_sum(q * q, ones_bd) + 1e-12) * (GDN_HEAD_DIM ** -0.5)
    k = k * lax.rsqrt(_dot_sum(k * k, ones_bd) + 1e-12)
    rest = cur_ref[0, :, 4 * dg:]
    beta_k = jax.nn.sigmoid(_dot_xr(rest, selb_k[...]))
    g_i = na_i[...] * _softplus(_dot_xr(rest, selg_i[...]) + dt_i[...])
    g_k = na_k[...] * _softplus(_dot_xr(rest, selg_k[...]) + dt_k[...])

    mk = _chunk_masks(rev)
    kd = GDN_HEAD_DIM
    mk['eye_k'] = _f(lax.broadcasted_iota(jnp.int32, (kd, kd), 0) == lax.broadcasted_iota(jnp.int32, (kd, kd), 1))
    mk['bd_kt'] = _f(_block_diag_mask(dg, HEADS * CHUNK, kd, CHUNK)).astype(BF16)
    mk['bd_x'] = _head_col_mask(HEADS * CHUNK, 2 * dg, CHUNK, kd)
    nch = tm // CHUNK
    order = list(range(nch - 1, -1, -1) if rev else range(nch))
    outs = _gdn_chunks(q, k, v, beta_k, g_i, g_k, s_ref, mk, order)
    for ci, o in zip(order, outs):
        o_ref[0, ci * CHUNK:(ci + 1) * CHUNK, :] = o


def _gdn_dir(p, conv_w, consts, nct, rev, dg):
    B, T, cols = p.shape
    tm = TOKEN_TILE
    n_tiles = T // tm
    hl = GDN_CONV_HALO
    hpt = tm // hl
    nhb = T // hl
    c3 = 3 * dg
    tile = lambda s: _tile_of_step(s, nct, n_tiles, rev)
    full = lambda a: pl.BlockSpec(a.shape, lambda b, s: (0,) * a.ndim)
    return pl.pallas_call(
        functools.partial(_gdn_dir_kernel, rev=rev, nct=nct, n_tiles=n_tiles, dg=dg),
        out_shape=jax.ShapeDtypeStruct((B, T, dg), F32),
        grid=(B, n_tiles),
        in_specs=[
            pl.BlockSpec((1, hl, c3), lambda b, s: (b, jnp.maximum(tile(s) * hpt - 1, 0), 0)),
            pl.BlockSpec((1, tm, cols), lambda b, s: (b, tile(s), 0)),
            pl.BlockSpec((1, hl, c3), lambda b, s: (b, jnp.minimum((tile(s) + 1) * hpt, nhb - 1), 0)),
            full(conv_w)] + [full(a) for a in consts],
        out_specs=pl.BlockSpec((1, tm, dg), lambda b, s: (b, tile(s), 0)),
        scratch_shapes=[pltpu.VMEM((tm + 2 * hl, c3), F32), pltpu.VMEM((HEADS, GDN_HEAD_DIM, GDN_HEAD_DIM), F32)],
        compiler_params=_cparams(("parallel", "arbitrary")),
        name="gdn_rev" if rev else "gdn_fwd",
    )(p, p, p, conv_w, *consts)


def _gdn_consts(a_log_d, dt_bias_d, d, dg):
    H = HEADS
    ncol = 4 * H
    def sel(base, width):
        m = np.zeros((ncol, H * width), np.float32)
        for h in range(H):
            m[base + d * H + h, h * width:(h + 1) * width] = 1.0
        return jnp.asarray(m, BF16)
    neg_a = -jnp.exp(a_log_d)
    exp_i = lambda x: jnp.repeat(x, CHUNK)[None, :]
    exp_k = lambda x: jnp.repeat(x, GDN_HEAD_DIM)[None, :]
    return (sel(0, GDN_HEAD_DIM), sel(2 * H, CHUNK), sel(2 * H, GDN_HEAD_DIM),
            exp_i(neg_a), exp_k(neg_a), exp_i(dt_bias_d), exp_k(dt_bias_d))


def _out_proj_kernel(h_ref, hy_ref, ro0_ref, ro1_ref, bv0_ref, bv1_ref, rg_ref, go0_ref, go1_ref, z_ref,
                     gnw_ref, gnb_ref, gdw_ref, w_ref, gate_ref, shift_ref, scale_ref, nw_ref, rwt_ref, rb_ref,
                     hn_ref, n_ref, lg_ref, *, splits):
    c0, c1, c2 = splits
    o = ro0_ref[0] + ro1_ref[0]
    ones_r = _f(_block_diag_mask(c1, c1, RWKV_HEAD_DIM, RWKV_HEAD_DIM)).astype(BF16)
    mean = _dot_sum(o, ones_r) * (1.0 / RWKV_HEAD_DIM)
    cen = o - mean
    var = _dot_sum(cen * cen, ones_r) * (1.0 / RWKV_HEAD_DIM)
    y_rw = (cen * lax.rsqrt(var + RWKV_GN_EPS) * gnw_ref[...] + gnb_ref[...] + bv0_ref[0] + bv1_ref[0]) * rg_ref[0]
    og = go0_ref[0] + go1_ref[0]
    ones_g = _f(_block_diag_mask(c2, c2, GDN_HEAD_DIM, GDN_HEAD_DIM)).astype(BF16)
    ms = _dot_sum(og * og, ones_g) * (1.0 / GDN_HEAD_DIM)
    z = z_ref[0]
    y_gd = og * lax.rsqrt(ms + NORM_EPS) * gdw_ref[...] * (z * jax.nn.sigmoid(z))

    acc = _dot(hy_ref[0].astype(BF16), w_ref[:c0, :])
    acc += _dot(y_rw.astype(BF16), w_ref[c0:c0 + c1, :])
    acc += _dot(y_gd.astype(BF16), w_ref[c0 + c1:c0 + c1 + c2, :])
    h = h_ref[0] + gate_ref[0, 0] * acc
    hn_ref[0] = h
    n = h * lax.rsqrt(jnp.mean(h * h, axis=-1, keepdims=True) + NORM_EPS) * nw_ref[...]
    n = n * (1.0 + scale_ref[0, 0]) + shift_ref[0, 0]
    n_ref[0] = n
    lg_ref[0] = _dot3(n, rwt_ref[...]) + rb_ref[...]


def _out_proj(h, y_hy, rw_parts, gd_parts, p_gd, gn_w, gn_b, gd_w, w_bf, gate, shift, scale, norm_w,
              router_w, router_b, n_ctx_tiles):
    B, T, D = h.shape
    tm = TOKEN_TILE
    E = router_w.shape[1]
    splits = (y_hy.shape[-1], rw_parts[0].shape[-1], gd_parts[0].shape[-1])
    c2 = splits[2]
    seg = lambda b, i: (b, (i >= n_ctx_tiles).astype(jnp.int32), 0, 0)
    tok = lambda c: pl.BlockSpec((1, tm, c), lambda b, i: (b, i, 0))
    full = lambda s: pl.BlockSpec(s, lambda b, i: (0,) * len(s))
    z_spec = pl.BlockSpec((1, tm, c2), lambda b, i: (b, i, 3))
    return pl.pallas_call(
        functools.partial(_out_proj_kernel, splits=splits),
        out_shape=[jax.ShapeDtypeStruct((B, T, D), F32), jax.ShapeDtypeStruct((B, T, D), F32),
                   jax.ShapeDtypeStruct((B, T, E), F32)],
        grid=(B, T // tm),
        in_specs=[tok(D), tok(splits[0])] + [tok(splits[1])] * 5 + [tok(c2), tok(c2), z_spec,
                  full((1, splits[1])), full((1, splits[1])), full((1, c2)), full(w_bf.shape),
                  pl.BlockSpec((1, 1, 1, D), seg), pl.BlockSpec((1, 1, 1, D), seg),
                  pl.BlockSpec((1, 1, 1, D), seg), full((1, D)), full(router_w.shape), full((1, E))],
        out_specs=[tok(D), tok(D), tok(E)],
        compiler_params=_cparams(("parallel", "parallel")),
        name="out_proj",
    )(h, y_hy, *rw_parts, *gd_parts, p_gd, gn_w, gn_b, gd_w, w_bf, gate, shift, scale, norm_w, router_w, router_b)


def _short_conv(u, w, b):
    L = u.shape[0]
    row = lax.broadcasted_iota(jnp.int32, (L, 1), 0)
    prev = jnp.where(row == 0, 0.0, pltpu.roll(u, 1, axis=0))
    nxt = jnp.where(row == L - 1, 0.0, pltpu.roll(u, L - 1, axis=0))
    return prev * w[0:1] + u * w[1:2] + nxt * w[2:3] + b


def _hyena_segment(u_ref, rows, sw_ref, sb_ref, skip_ref, cm_ref, sm_ref, p_ref, q_ref, pn_ref, d):
    conv = lambda g: _short_conv(u_ref[0, rows, g * d:(g + 1) * d], sw_ref[:, g * d:(g + 1) * d],
                                 sb_ref[:, g * d:(g + 1) * d])
    v = conv(2) * conv(1)
    vb = v.astype(BF16)
    a = _dot(cm_ref[...], vb)
    b = _dot(sm_ref[...], vb)
    p = p_ref[...]
    q = q_ref[...]
    yr = (a * p - b * q).astype(BF16)
    yi = (a * q + b * p).astype(BF16)
    y = _dot(cm_ref[...], yr) + _dot(sm_ref[...], yi)
    L = v.shape[0]
    sgn = jnp.where((lax.broadcasted_iota(jnp.int32, (L, 1), 0) & 1) == 0, 1.0, -1.0)
    a_nyq = jnp.sum(v * sgn, axis=0, keepdims=True)
    y = y + sgn * (a_nyq * pn_ref[...])
    return (y + v * skip_ref[...]) * conv(0)


def _hyena_kernel(u_ref, sw_ref, sb_ref, skip_ref, cm_l, sm_l, p_l, q_l, pn_l, cm_c, sm_c, p_c, q_c, pn_c,
                  y_ref, *, lc, with_ctx, d):
    T = u_ref.shape[1]
    lat = slice(lc, T)
    ctx = slice(0, lc)
    y_ref[0, lat, :] = _hyena_segment(u_ref, lat, sw_ref, sb_ref, skip_ref, cm_l, sm_l, p_l, q_l, pn_l, d)
    if with_ctx:
        y_ref[0, ctx, :] = _hyena_segment(u_ref, ctx, sw_ref, sb_ref, skip_ref, cm_c, sm_c, p_c, q_c, pn_c, d)
    else:
        y_ref[0, ctx, :] = jnp.zeros((lc, d), F32)


def _hyena(u, short_w, short_b, skip, lat_consts, ctx_consts, lc, with_ctx):
    B, T, cols = u.shape
    d = cols // 3
    one = pl.Buffered(1)
    full = lambda a: pl.BlockSpec(a.shape, lambda b: (0,) * a.ndim, pipeline_mode=one)
    consts = (short_w, short_b, skip) + tuple(lat_consts) + tuple(ctx_consts)
    return pl.pallas_call(
        functools.partial(_hyena_kernel, lc=lc, with_ctx=with_ctx, d=d),
        out_shape=jax.ShapeDtypeStruct((B, T, d), F32),
        grid=(B,),
        in_specs=[pl.BlockSpec((1, T, cols), lambda b: (b, 0, 0), pipeline_mode=one)] + [full(a) for a in consts],
        out_specs=pl.BlockSpec((1, T, d), lambda b: (b, 0, 0)),
        compiler_params=_cparams(("parallel",)),
        name="hyena",
    )(u, *consts)


def _moe_kernel(be_ref, nu_ref, x_ref, wgu_ref, bgu_ref, wdn_ref, bdn_ref, y_ref, wgu_bf, wdn_bf, *, d_expert):
    i = pl.program_id(0)
    prev = be_ref[jnp.maximum(i - 1, 0)]
    changed = jnp.logical_or(i == 0, be_ref[i] != prev)

    @pl.when(changed)
    def _():
        wgu_bf[...] = wgu_ref[0, 0].astype(BF16)
        wdn_bf[...] = wdn_ref[0, 0].astype(BF16)

    @pl.when(i < nu_ref[0])
    def _():
        gu = _dot(x_ref[...].astype(BF16), wgu_bf[...]) + bgu_ref[0, 0]
        gate = jnp.minimum(gu[:, :d_expert], SWIGLU_LIMIT)
        up = jnp.clip(gu[:, d_expert:], -SWIGLU_LIMIT, SWIGLU_LIMIT)
        glu = gate * jax.nn.sigmoid(gate * SWIGLU_ALPHA)
        act = ((up + 1.0) * glu).astype(BF16)
        y_ref[...] = _dot(act, wdn_bf[...]) + bdn_ref[0, 0]

    @pl.when(i >= nu_ref[0])
    def _():
        y_ref[...] = jnp.zeros_like(y_ref)


def _moe_experts(xs, block_e, n_used, w_gu, b_gu, w_dn, b_dn, layer):
    NP, D = xs.shape
    depth, E, _, F2 = w_gu.shape
    Fe = F2 // 2
    tm = MOE_TILE
    nb = NP // tm
    grid_spec = pltpu.PrefetchScalarGridSpec(
        num_scalar_prefetch=2,
        grid=(nb,),
        in_specs=[
            pl.BlockSpec((tm, D), lambda i, be, nu: (i, 0)),
            pl.BlockSpec((1, 1, D, F2), lambda i, be, nu: (layer, be[i], 0, 0)),
            pl.BlockSpec((1, 1, 1, F2), lambda i, be, nu: (layer, be[i], 0, 0)),
            pl.BlockSpec((1, 1, Fe, D), lambda i, be, nu: (layer, be[i], 0, 0)),
            pl.BlockSpec((1, 1, 1, D), lambda i, be, nu: (layer, be[i], 0, 0)),
        ],
        out_specs=pl.BlockSpec((tm, D), lambda i, be, nu: (i, 0)),
        scratch_shapes=[pltpu.VMEM((D, F2), BF16), pltpu.VMEM((Fe, D), BF16)],
    )
    return pl.pallas_call(
        functools.partial(_moe_kernel, d_expert=Fe),
        out_shape=jax.ShapeDtypeStruct((NP, D), F32),
        grid_spec=grid_spec,
        compiler_params=_cparams(("arbitrary",)),
        name="moe_experts",
    )(block_e, n_used, xs, w_gu, b_gu.reshape(depth, E, 1, F2), w_dn, b_dn.reshape(depth, E, 1, D))


def _combine_kernel(h_ref, yk_ref, rg_ref, g_ref, w_ref, o_ref, *, final):
    rg = rg_ref[0]
    y = yk_ref[0, 0].astype(F32) * rg[:, 0:1]
    for k in range(1, TOP_K):
        y += yk_ref[k, 0].astype(F32) * rg[:, k:k + 1]
    h = h_ref[0] + g_ref[0, 0] * y
    if final:
        h = h * lax.rsqrt(jnp.mean(h * h, axis=-1, keepdims=True) + NORM_EPS) * w_ref[...]
    o_ref[0] = h


def _combine(h, yk, route_gates, gate, w, n_ctx_tiles, final):
    B, T, D = h.shape
    tm = TOKEN_TILE
    tok = lambda c: pl.BlockSpec((1, tm, c), lambda b, i: (b, i, 0))
    yk_spec = pl.BlockSpec((TOP_K, 1, tm, D), lambda b, i: (0, b, i, 0))
    nseg = gate.shape[1]
    seg = lambda b, i: (b, jnp.minimum((i >= n_ctx_tiles).astype(jnp.int32), nseg - 1), 0, 0)
    return pl.pallas_call(
        functools.partial(_combine_kernel, final=final),
        out_shape=jax.ShapeDtypeStruct((B, T, D), F32),
        grid=(B, T // tm),
        in_specs=[tok(D), yk_spec, tok(TOP_K), pl.BlockSpec((1, 1, 1, D), seg),
                  pl.BlockSpec((1, D), lambda b, i: (0, 0))],
        out_specs=tok(D),
        compiler_params=_cparams(("parallel", "parallel")),
        name="combine",
    )(h, yk, route_gates, gate, w)


def _dft_mats(L):
    n2 = 2 * L
    f = lax.broadcasted_iota(jnp.int32, (L, L), 0)
    t = lax.broadcasted_iota(jnp.int32, (L, L), 1)
    ang = ((f * t) % n2).astype(F32) * (2.0 * math.pi / n2)
    return jnp.cos(ang), jnp.sin(ang)


def _hyena_filter(L, w1, b1, w2, b2, w3, b3, w4, freq, d_hy):
    pos = jnp.arange(L, dtype=F32)
    t = jnp.linspace(0.0, 1.0, L, dtype=F32)[:, None]
    bands = jnp.linspace(1e-4, HYENA_BANDS - 1, HYENA_BANDS, dtype=F32)
    ang = (2.0 * math.pi / L) * pos[:, None] * bands
    z = jnp.concatenate([t, jnp.cos(ang), -jnp.sin(ang)], axis=-1)
    hdn = jnp.sin(freq * (jnp.dot(z, w1, precision=_HI) + b1))
    hdn = jnp.sin(freq * (jnp.dot(hdn, w2, precision=_HI) + b2))
    hdn = jnp.sin(freq * (jnp.dot(hdn, w3, precision=_HI) + b3))
    h = jnp.dot(hdn, w4, precision=_HI)
    deltas = jnp.linspace(math.log(HYENA_DECAY_TARGET) / HYENA_SLOW_PCT,
                          math.log(HYENA_DECAY_TARGET) / HYENA_FAST_PCT, d_hy, dtype=F32)
    window = jnp.exp(-t * jnp.abs(deltas))
    h_fwd = h[:, :d_hy] * window
    h_bwd = h[:, d_hy:] * window
    k_lo = h_fwd
    k_hi = jnp.concatenate([jnp.zeros((1, d_hy), F32), h_bwd[:0:-1]], axis=0)
    return k_lo, k_hi


def _hyena_consts(dft, fparams, d_hy):
    cm, sm = dft
    L = cm.shape[0]
    k_lo, k_hi = _hyena_filter(L, *fparams, d_hy)
    sgn = jnp.where(jnp.arange(L) % 2 == 0, 1.0, -1.0).astype(F32)[:, None]
    wf = jnp.where(jnp.arange(L) == 0, 1.0, 2.0).astype(F32)[:, None] / (2 * L)
    cb, sb = cm.astype(BF16), sm.astype(BF16)
    klo, khi = k_lo.astype(BF16), k_hi.astype(BF16)
    p = (_dot(cb, klo) + sgn * _dot(cb, khi)) * wf
    q = (_dot(sb, klo) + sgn * _dot(sb, khi)) * wf
    pn = jnp.sum(sgn * (k_lo + k_hi), axis=0, keepdims=True) / (2 * L)
    return cm.astype(BF16), sm.astype(BF16), p, q, pn


def _pad_rows(w, lo, total):
    return jnp.zeros((total, w.shape[1]), w.dtype).at[lo:lo + w.shape[0]].set(w)


def _rwkv_mixer_parts(p, mu, w0, w2, a0, a2, g2, k_k, k_a, r_k, nct):
    row = lambda x: x[None, :]
    lw_w, lw_a = w2.shape[1], a2.shape[1]
    outs = []
    for d in range(2):
        w2p = _pad_rows(w2[d], d * lw_w, 2 * lw_w)
        a2p = _pad_rows(a2[d], d * lw_a, 2 * lw_a)
        outs.append(_rwkv_dir(p, row(mu), w2p, row(w0[d]), a2p, row(a0[d]), g2, row(k_k), row(k_a), row(r_k),
                              nct, rev=bool(d)))
    (o0, bv0, g), (o1, bv1) = outs
    return o0, o1, bv0, bv1, g


def _gdn_mixer_parts(p, conv_w, a_log, dt_bias, nct, dg):
    return [_gdn_dir(p, conv_w, _gdn_consts(a_log[d], dt_bias[d], d, dg), nct, rev=bool(d), dg=dg) for d in range(2)]


def _moe(n_bf, logits, w_gu, b_gu, w_dn, b_dn, layer):
    Nt, D = n_bf.shape
    E = w_gu.shape[1]
    tm = MOE_TILE
    top_logit, top_idx = lax.top_k(logits, TOP_K)
    gates = jax.nn.softmax(top_logit, axis=-1)
    A = Nt * TOP_K
    flat_e = top_idx.T.reshape(A)
    experts = jnp.arange(E, dtype=jnp.int32)
    counts = jnp.sum((flat_e[:, None] == experts[None, :]).astype(jnp.int32), axis=0)
    blocks_per = (counts + tm - 1) // tm
    block_end = jnp.cumsum(blocks_per)
    n_blocks = -(-A // tm) + E
    n_slots = n_blocks * tm
    block_e = jnp.minimum(jnp.sum((block_end[None, :] <= jnp.arange(n_blocks)[:, None]).astype(jnp.int32), axis=1),
                          E - 1).astype(jnp.int32)
    n_used = block_end[-1:].astype(jnp.int32)
    need = blocks_per * tm - counts
    pad_keys = jnp.where(jnp.arange(tm - 1)[None, :] < need[:, None], experts[:, None], E).reshape(E * (tm - 1))
    tail = jnp.full((n_slots - A - E * (tm - 1),), E, jnp.int32)
    keys = jnp.concatenate([flat_e, pad_keys, tail])
    ids = jnp.arange(n_slots, dtype=jnp.int32)
    _, slot_id = lax.sort((keys, ids), num_keys=1)
    _, slot_of_id = lax.sort((slot_id, ids), num_keys=1)
    dest = slot_of_id[:A]
    slot_tok = jnp.where(slot_id < A, slot_id % Nt, 0)
    xs = n_bf[slot_tok]
    yb = _moe_experts(xs, block_e, n_used, w_gu, b_gu, w_dn, b_dn, layer)
    return yb[dest.reshape(TOP_K, Nt)], gates


def kernel(x, c, ctx, c_ctx, ada_w, ada_b, norm_mix_w, norm_ffn_w, final_norm_w, w_in, w_out, hy_short_w, hy_short_b, hy_f_w1, hy_f_b1, hy_f_w2, hy_f_b2, hy_f_w3, hy_f_b3, hy_f_w4, hy_f_freq, hy_skip, rw_mu, rw_w0, rw_w2, rw_a0, rw_a2, rw_g2, rw_k_k, rw_k_a, rw_r_k, rw_gn_w, rw_gn_b, gdn_conv_w, gdn_a_log, gdn_dt_bias, gdn_norm_w, moe_router_w, moe_router_b, moe_w_gu, moe_b_gu, moe_w_dn, moe_b_dn):
    B, L, D = x.shape
    Lc = ctx.shape[1]
    depth = ada_w.shape[0]
    d_hy = hy_skip.shape[-1]
    hy_cols = 3 * d_hy
    rw_cols = rw_mu.shape[-1]
    gd_cols = w_in.shape[-1] - hy_cols - rw_cols
    dg = gdn_conv_w.shape[-1] // 3
    assert gdn_a_log.shape[-1] == HEADS and rw_w0.shape[-1] == HEADS * RWKV_HEAD_DIM and dg == HEADS * GDN_HEAD_DIM
    assert Lc % TOKEN_TILE == 0 and L % TOKEN_TILE == 0 and TOKEN_TILE % GRID_W == 0
    splits = (hy_cols, rw_cols, gd_cols)
    nct = Lc // TOKEN_TILE
    row = lambda v: v[None, :]

    h = jnp.concatenate([ctx, x], axis=1)
    s_lat = jax.nn.silu(c)
    s_ctx = jax.nn.silu(c_ctx)
    dft_lat = _dft_mats(L)
    dft_ctx = _dft_mats(Lc)
    mods = None
    for l in range(depth):
        last = l == depth - 1
        ml = jnp.dot(s_lat, ada_w[l], precision=_HI) + ada_b[l]
        mc = jnp.dot(s_ctx, ada_w[l], precision=_HI) + ada_b[l]
        mods = jnp.stack([jnp.broadcast_to(mc, ml.shape), ml], axis=1).reshape(B, 2, N_MOD, 1, D)
        mod = lambda j: mods[:, :, j]
        p_hy, p_rw, p_gd = _in_proj(h, mod(0), mod(1), row(norm_mix_w[l]), w_in[l].astype(BF16), splits, nct)

        fparams = (hy_f_w1[l], hy_f_b1[l], hy_f_w2[l], hy_f_b2[l], hy_f_w3[l], hy_f_b3[l], hy_f_w4[l], hy_f_freq[l])
        y_hy = _hyena(p_hy, hy_short_w[l], row(hy_short_b[l]), row(hy_skip[l]), _hyena_consts(dft_lat, fparams, d_hy),
                      _hyena_consts(dft_ctx, fparams, d_hy), Lc, with_ctx=not last)
        rw_parts = _rwkv_mixer_parts(p_rw, rw_mu[l], rw_w0[l], rw_w2[l], rw_a0[l], rw_a2[l], rw_g2[l],
                                     rw_k_k[l], rw_k_a[l], rw_r_k[l], nct)
        gd_parts = _gdn_mixer_parts(p_gd, gdn_conv_w[l], gdn_a_log[l], gdn_dt_bias[l], nct, dg)

        h, n_bf, logits = _out_proj(h, y_hy, rw_parts, gd_parts, p_gd, row(rw_gn_w[l]), row(rw_gn_b[l]),
                                    row(jnp.tile(gdn_norm_w[l], HEADS)), w_out[l].astype(BF16), mod(2), mod(3),
                                    mod(4), row(norm_ffn_w[l]), moe_router_w[l], row(moe_router_b[l]), nct)
        if last:
            h, n_bf, logits = h[:, Lc:], n_bf[:, Lc:], logits[:, Lc:]
        Tm = h.shape[1]
        yk, route_gates = _moe(n_bf.reshape(B * Tm, D), logits.reshape(B * Tm, -1),
                               moe_w_gu, moe_b_gu, moe_w_dn, moe_b_dn, l)
        h = _combine(h, yk.reshape(TOP_K, B, Tm, D), route_gates.reshape(B, Tm, TOP_K),
                     mods[:, 1:2, 5] if last else mod(5), row(final_norm_w), 0 if last else nct, final=last)
    return h
```

```python
import functools
import math

import jax
import jax.numpy as jnp
import numpy as np
from jax import lax
from jax.experimental import pallas as pl
from jax.experimental.pallas import tpu as pltpu

F32 = jnp.float32
BF16 = jnp.bfloat16

NORM_EPS = 1e-6
N_MOD = 6
GRID_W = 64
CHUNK = 64
HEADS = 4
HYENA_BANDS = 16
HYENA_DECAY_TARGET = 1e-2
HYENA_FAST_PCT = 0.3
HYENA_SLOW_PCT = 1.5
RWKV_HEAD_DIM = 64
RWKV_GN_EPS = 64e-5
GDN_HEAD_DIM = 128
GDN_CONV_HALO = 8
TOP_K = 4
SWIGLU_ALPHA = 1.702
SWIGLU_LIMIT = 7.0

TOKEN_TILE = 256
MOE_TILE = 256
VMEM_LIMIT = 56 * 1024 * 1024
_HI = lax.Precision.HIGHEST


def _cparams(sem):
    return pltpu.CompilerParams(dimension_semantics=sem, vmem_limit_bytes=VMEM_LIMIT)


def _dot(a, b):
    return jnp.dot(a, b, preferred_element_type=F32)


def _split2(x):
    hi = x.astype(BF16)
    lo = (x - hi.astype(F32)).astype(BF16)
    return hi, lo


def _split3(x):
    hi = x.astype(BF16)
    r1 = x - hi.astype(F32)
    mid = r1.astype(BF16)
    lo = (r1 - mid.astype(F32)).astype(BF16)
    return hi, mid, lo


def _dot3(a, b):
    ah, al = _split2(a)
    bh, bl = _split2(b)
    return _dot(ah, bh) + _dot(al, bh) + _dot(ah, bl)


def _dot_xl(sel_bf, x):
    hi, mid, lo = _split3(x)
    return _dot(sel_bf, hi) + _dot(sel_bf, mid) + _dot(sel_bf, lo)


def _dot_xr(x, sel_bf):
    hi, mid, lo = _split3(x)
    return _dot(hi, sel_bf) + _dot(mid, sel_bf) + _dot(lo, sel_bf)


def _dot_sum(x, ones_bf):
    hi, lo = _split2(x)
    return _dot(hi, ones_bf) + _dot(lo, ones_bf)


def _f(mask):
    return jnp.where(mask, 1.0, 0.0).astype(F32)


def _block_diag_mask(rows, cols, rblk, cblk):
    r = lax.broadcasted_iota(jnp.int32, (rows, cols), 0) >> int(math.log2(rblk))
    c = lax.broadcasted_iota(jnp.int32, (rows, cols), 1) >> int(math.log2(cblk))
    return r == c


def _softplus(x):
    return jnp.maximum(x, 0.0) + jnp.log1p(jnp.exp(-jnp.abs(x)))


def _chunk_masks(rev):
    shape = (CHUNK, HEADS * CHUNK)
    i = lax.broadcasted_iota(jnp.int32, shape, 0)
    j = lax.broadcasted_iota(jnp.int32, shape, 1) & (CHUNK - 1)
    strict = (j > i) if rev else (j < i)
    incl = (j >= i) if rev else (j <= i)
    same16 = (i >> 4) == (j >> 4)
    same32 = (i >> 5) == (j >> 5)
    ii = lax.broadcasted_iota(jnp.int32, (CHUNK, CHUNK), 0)
    jj = lax.broadcasted_iota(jnp.int32, (CHUNK, CHUNK), 1)
    tri = _f((jj >= ii) if rev else (jj <= ii)).astype(BF16)
    bdf = _f(_block_diag_mask(HEADS * CHUNK, HEADS * CHUNK, CHUNK, CHUNK))
    return dict(rev=rev, strict=_f(strict), incl=_f(incl), incl_b=incl, eye=_f(i == j), m16=_f(same16),
                m32=_f(jnp.logical_and(same32, jnp.logical_not(same16))), m64=_f(jnp.logical_not(same32)),
                tri=tri, bdf=bdf, bd=bdf.astype(BF16))


def _head_col_mask(rows, cols, rblk, cblk):
    r = lax.broadcasted_iota(jnp.int32, (rows, cols), 0) >> int(math.log2(rblk))
    c = (lax.broadcasted_iota(jnp.int32, (rows, cols), 1) >> int(math.log2(cblk))) & (HEADS - 1)
    return _f(r == c).astype(BF16)


def _mmc(a, b):
    return _dot(a.astype(BF16), b.astype(BF16))


def _bdiag(y, bd):
    return jnp.concatenate([y.astype(BF16)] * HEADS, axis=0) * bd


def _bdiag_t(y, bd):
    return jnp.concatenate([y] * HEADS, axis=0).T.astype(BF16) * bd


def _tri_inv_each(a_list, mk):
    bd = mk['bd']
    mmh = lambda xs, ys: [_mmc(x, _bdiag(y, bd)) for x, y in zip(xs, ys)]
    d = [a * mk['m16'] for a in a_list]
    e = [a * mk['m32'] for a in a_list]
    f = [a * mk['m64'] for a in a_list]
    d2 = mmh(d, d)
    d4 = mmh(d2, d2)
    d8 = mmh(d4, d4)
    t = [mk['eye'] + x for x in d]
    for p in (d2, d4, d8):
        t = [x + y for x, y in zip(t, mmh(t, p))]
    for p in (e, f):
        t = [x + y for x, y in zip(t, mmh(mmh(t, p), t))]
    return t


def _tile_of_step(s, nct, n_tiles, rev):
    if not rev:
        return s
    return jnp.where(s < nct, nct - 1 - s, n_tiles - 1 - (s - nct))


def _in_proj_kernel(h_ref, shift_ref, scale_ref, nw_ref, w_ref, hy_ref, rw_ref, gd_ref, *, splits):
    h = h_ref[0]
    n = h * lax.rsqrt(jnp.mean(h * h, axis=-1, keepdims=True) + NORM_EPS) * nw_ref[...]
    n = (n * (1.0 + scale_ref[0, 0]) + shift_ref[0, 0]).astype(BF16)
    c0, c1, c2 = splits
    hy_ref[0] = _dot(n, w_ref[:, :c0])
    rw_ref[0] = _dot(n, w_ref[:, c0:c0 + c1])
    gd_ref[0] = _dot(n, w_ref[:, c0 + c1:c0 + c1 + c2])


def _in_proj(h, shift, scale, norm_w, w_bf, splits, n_ctx_tiles):
    B, T, D = h.shape
    tm = TOKEN_TILE
    seg = lambda b, i: (b, (i >= n_ctx_tiles).astype(jnp.int32), 0, 0)
    cols = w_bf.shape[1]
    outs = [jax.ShapeDtypeStruct((B, T, c), F32) for c in splits]
    return pl.pallas_call(
        functools.partial(_in_proj_kernel, splits=splits),
        out_shape=outs,
        grid=(B, T // tm),
        in_specs=[
            pl.BlockSpec((1, tm, D), lambda b, i: (b, i, 0)),
            pl.BlockSpec((1, 1, 1, D), seg),
            pl.BlockSpec((1, 1, 1, D), seg),
            pl.BlockSpec((1, D), lambda b, i: (0, 0)),
            pl.BlockSpec((D, cols), lambda b, i: (0, 0)),
        ],
        out_specs=[pl.BlockSpec((1, tm, c), lambda b, i: (b, i, 0)) for c in splits],
        compiler_params=_cparams(("parallel", "parallel")),
        name="in_proj",
    )(h, shift, scale, norm_w, w_bf)


def _rwkv_chunks(r, lw, k, v, a, b, s_ref, mk, order):
    n = len(order)
    hk = r.shape[1]
    rows = [slice(ci * CHUNK, (ci + 1) * CHUNK) for ci in order]
    each = lambda fn, *ls: [fn(*xs) for xs in zip(*ls)]
    pick = lambda x: [x[s] for s in rows]
    r, lw, k, v, a, b = pick(r), pick(lw), pick(k), pick(v), pick(a), pick(b)
    bd, bd2 = mk['bd'], mk['bd2']
    c_all = _dot_xl(mk['tri'], jnp.concatenate(lw, axis=1))
    c = [c_all[:, i * hk:(i + 1) * hk] for i in range(n)]
    end = 0 if mk['rev'] else CHUNK - 1
    c_end = [x[end:end + 1] for x in c]
    rt = each(lambda r_, c_: r_ * jnp.exp(c_), r, c)
    at = each(lambda a_, c_, lw_: a_ * jnp.exp(c_ - lw_), a, c, lw)
    enc = each(lambda c_: jnp.exp(-c_), c)
    bt = each(lambda b_, e_: b_ * e_, b, enc)
    kt = each(lambda k_, e_: k_ * e_, k, enc)
    e2 = each(lambda c_, ce: jnp.exp(ce - c_), c, c_end)
    bb = each(lambda b_, e_: b_ * e_, b, e2)
    kb = each(lambda k_, e_: k_ * e_, k, e2)
    btk = each(lambda bt_, kt_: jnp.concatenate(
        [_bdiag_t(bt_, bd), _bdiag_t(kt_, bd)], axis=1), bt, kt)
    res = each(lambda at_, rt_, w_: _mmc(jnp.concatenate([at_, rt_], axis=0), w_), at, rt, btk)
    hc = HEADS * CHUNK
    a_ab = [x[:CHUNK, :hc] * mk['strict'] for x in res]
    a_ak = [x[:CHUNK, hc:] * mk['strict'] for x in res]
    a_rb = [x[CHUNK:, :hc] * mk['incl'] for x in res]
    a_rk = [x[CHUNK:, hc:] * mk['incl'] for x in res]
    t = _tri_inv_each(a_ab, mk)
    vbd = each(lambda v_: _bdiag(v_, bd), v)
    akv = each(_mmc, a_ak, vbd)
    au = each(lambda t_, at_, akv_: _mmc(t_, _bdiag(jnp.concatenate([at_, akv_], axis=1), bd2)), t, at, akv)
    aubd = each(lambda x: _bdiag(x, bd2), au)
    ru = each(_mmc, a_rb, aubd)
    rh = each(lambda rt_, x: rt_ + x[:, :hk], rt, ru)
    oloc = each(lambda a_, vb, x: _mmc(a_, vb) + x[:, hk:], a_rk, vbd, ru)
    mn = each(lambda bb_, kb_, au_, v_: _mmc(
        jnp.concatenate([bb_, kb_], axis=0).T,
        jnp.concatenate([au_, jnp.concatenate([jnp.zeros_like(v_), v_], axis=1)], axis=0)), bb, kb, au, v)
    mt = each(lambda x, ce: x[:, :hk] * mk['bdf'] + mk['eye_k'] * jnp.exp(ce), mn, c_end)
    nt = [x[:, hk:] * mk['bdf'] for x in mn]
    outs = []
    s = s_ref[...]
    for i in range(n):
        outs.append(_mmc(rh[i], s) + oloc[i])
        s = _dot3(mt[i], s) + nt[i]
    s_ref[...] = s
    return outs


def _rwkv_dir_kernel(prev_ref, cur_ref, next_ref, mu_ref, w2_ref, w0_ref, a2_ref, a0_ref, g2_ref, kk_ref,
                     ka_ref, rk_ref, o_ref, bv_ref, *rest, rev, nct, n_tiles, dr):
    g_ref = rest[0] if len(rest) == 3 else None
    buf, s_ref = rest[-2:]
    step = pl.program_id(1)
    t = _tile_of_step(step, nct, n_tiles, rev)
    tm = TOKEN_TILE
    hw = GRID_W

    @pl.when(step == 0)
    def _():
        s_ref[...] = jnp.zeros_like(s_ref)

    is_ctx = t < nct
    first = jnp.logical_or(t == 0, t == nct)
    last = jnp.logical_or(t == nct - 1, t == n_tiles - 1)
    cur = cur_ref[0]
    buf[0:hw] = jnp.where(first, 0.0, prev_ref[0])
    buf[hw:hw + tm] = cur
    buf[hw + tm:hw + tm + hw] = jnp.where(last, 0.0, next_ref[0])
    cols = cur.shape[1]
    left = buf[hw - 1:hw - 1 + tm]
    right = buf[hw + 1:hw + 1 + tm]
    up = buf[0:tm]
    down = buf[2 * hw:2 * hw + tm]
    col = lax.broadcasted_iota(jnp.int32, (tm, 1), 0) & (hw - 1)
    lane = lax.broadcasted_iota(jnp.int32, (1, cols), 1)
    left_g = jnp.where(col == 0, 0.0, left)
    right_g = jnp.where(col == hw - 1, 0.0, right)
    l4 = lane & 3
    sh_lat = jnp.where(l4 == 0, left_g, jnp.where(l4 == 1, right_g, jnp.where(l4 == 2, up, down)))
    sh_ctx = jnp.where((lane & 1) == 0, left, right)
    sh = jnp.where(is_ctx, sh_ctx, sh_lat)
    m = cur + mu_ref[...] * (sh - cur)

    r = m[:, :dr]
    k = m[:, dr:2 * dr]
    v = m[:, 2 * dr:3 * dr]
    lw_w = w2_ref.shape[0]
    lw_a = a2_ref.shape[0]
    o3 = 3 * dr
    wd = m[:, o3:o3 + lw_w]
    ad = m[:, o3 + lw_w:o3 + lw_w + lw_a]
    gd = m[:, o3 + lw_w + lw_a:]
    lw = -_softplus(-(w0_ref[...] + _dot3(jnp.tanh(wd), w2_ref[...]))) - 0.5
    logw = -jnp.exp(lw)
    a_lr = jax.nn.sigmoid(a0_ref[...] + _dot3(ad, a2_ref[...]))
    ones_bd = _f(_block_diag_mask(dr, dr, RWKV_HEAD_DIM, RWKV_HEAD_DIM)).astype(BF16)
    kx = k * kk_ref[...]
    kk = kx * lax.rsqrt(_dot_sum(kx * kx, ones_bd) + 1e-12)
    k_d = k * (1.0 + (a_lr - 1.0) * ka_ref[...])
    b_d = kk * a_lr
    a_s = -kk
    bv_ref[0] = _dot_sum(r * k_d * rk_ref[...], ones_bd) * v
    if g_ref is not None:
        g_ref[0] = _dot3(jax.nn.sigmoid(gd), g2_ref[...])

    mk = _chunk_masks(rev)
    mk['eye_k'] = _f(lax.broadcasted_iota(jnp.int32, (dr, dr), 0) == lax.broadcasted_iota(jnp.int32, (dr, dr), 1))
    mk['bd2'] = _head_col_mask(HEADS * CHUNK, 2 * dr, CHUNK, RWKV_HEAD_DIM)
    nch = tm // CHUNK
    order = list(range(nch - 1, -1, -1) if rev else range(nch))
    outs = _rwkv_chunks(r, logw, k_d, v, a_s, b_d, s_ref, mk, order)
    for ci, o in zip(order, outs):
        o_ref[0, ci * CHUNK:(ci + 1) * CHUNK, :] = o


def _rwkv_dir(p, mu, w2p, w0, a2p, a0, g2, k_k, k_a, r_k, nct, rev):
    B, T, cols = p.shape
    tm = TOKEN_TILE
    n_tiles = T // tm
    dr = g2.shape[1]
    hpt = tm // GRID_W
    nhb = T // GRID_W
    tile = lambda s: _tile_of_step(s, nct, n_tiles, rev)
    full = lambda a: pl.BlockSpec(a.shape, lambda b, s: (0,) * a.ndim)
    tok = lambda c: pl.BlockSpec((1, tm, c), lambda b, s: (b, tile(s), 0))
    out_sds = jax.ShapeDtypeStruct((B, T, dr), F32)
    n_out = 2 if rev else 3
    return pl.pallas_call(
        functools.partial(_rwkv_dir_kernel, rev=rev, nct=nct, n_tiles=n_tiles, dr=dr),
        out_shape=[out_sds] * n_out,
        grid=(B, n_tiles),
        in_specs=[
            pl.BlockSpec((1, GRID_W, cols), lambda b, s: (b, jnp.maximum(tile(s) * hpt - 1, 0), 0)),
            tok(cols),
            pl.BlockSpec((1, GRID_W, cols), lambda b, s: (b, jnp.minimum((tile(s) + 1) * hpt, nhb - 1), 0)),
            full(mu), full(w2p), full(w0), full(a2p), full(a0), full(g2), full(k_k), full(k_a), full(r_k),
        ],
        out_specs=[tok(dr)] * n_out,
        scratch_shapes=[pltpu.VMEM((tm + 2 * GRID_W, cols), F32), pltpu.VMEM((dr, dr), F32)],
        compiler_params=_cparams(("parallel", "arbitrary")),
        name="rwkv_rev" if rev else "rwkv_fwd",
    )(p, p, p, mu, w2p, w0, a2p, a0, g2, k_k, k_a, r_k)


def _gdn_chunks(q, k, v, beta_k, g_i, g_k, s_ref, mk, order):
    n = len(order)
    kd = GDN_HEAD_DIM
    hk = HEADS * kd
    hc = HEADS * CHUNK
    rows = [slice(ci * CHUNK, (ci + 1) * CHUNK) for ci in order]
    each = lambda fn, *ls: [fn(*xs) for xs in zip(*ls)]
    pick = lambda x: [x[s] for s in rows]
    q, k, v, beta_k, g_i, g_k = pick(q), pick(k), pick(v), pick(beta_k), pick(g_i), pick(g_k)
    gi_all = _dot_xl(mk['tri'], jnp.concatenate(g_i, axis=1))
    gk_all = _dot_xl(mk['tri'], jnp.concatenate(g_k, axis=1))
    gc_i = [gi_all[:, i * hc:(i + 1) * hc] for i in range(n)]
    gc_k = [gk_all[:, i * hk:(i + 1) * hk] for i in range(n)]
    end = 0 if mk['rev'] else CHUNK - 1
    gl_k = [x[end:end + 1] for x in gc_k]
    incl = mk['incl_b']
    gc_j = each(lambda g: jnp.sum(g * mk['eye'], axis=0, keepdims=True), gc_i)
    decay = each(lambda gi, gj: jnp.where(incl, jnp.exp(jnp.where(incl, gi - gj, 0.0)), 0.0), gc_i, gc_j)
    kb = each(lambda k_, b_: k_ * b_, k, beta_k)
    ktT = each(lambda k_: _bdiag_t(k_, mk['bd_kt']), k)
    res = each(lambda kb_, q_, w_: _mmc(jnp.concatenate([kb_, q_], axis=0), w_), kb, q, ktT)
    m = each(lambda x, d_: x[:CHUNK] * d_ * mk['strict'], res, decay)
    attn = each(lambda x, d_: x[CHUNK:] * d_, res, decay)
    t = _tri_inv_each([-x for x in m], mk)
    egc = each(jnp.exp, gc_k)
    x = each(lambda v_, b_, kb_, e_: jnp.concatenate([v_ * b_, kb_ * e_], axis=1), v, beta_k, kb, egc)
    uw = each(lambda t_, x_: _mmc(t_, _bdiag(x_, mk['bd_x'])), t, x)
    auw = each(lambda a_, x_: _mmc(a_, _bdiag(x_, mk['bd_x'])), attn, uw)
    rh = each(lambda q_, e_, x_: q_ * e_ - x_[:, hk:], q, egc, auw)
    oloc = [x_[:, :hk] for x_ in auw]
    khT = each(lambda k_, gl, gc: jnp.concatenate([k_ * jnp.exp(gl - gc), jnp.zeros_like(k_)], axis=0).T,
               k, gl_k, gc_k)
    zero = jnp.zeros((CHUNK, 2 * kd), F32)
    heads = [slice(h * kd, (h + 1) * kd) for h in range(HEADS)]
    mn = [[_mmc(khT[i][sl], jnp.concatenate(
        [jnp.concatenate([uw[i][:, hk + h * kd:hk + (h + 1) * kd], uw[i][:, sl]], axis=1), zero], axis=0))
        for h, sl in enumerate(heads)] for i in range(n)]
    s = [s_ref[h] for h in range(HEADS)]
    outs = []
    for i in range(n):
        o_h = []
        for h, sl in enumerate(heads):
            o_h.append(_mmc(rh[i][:, sl], s[h]) + oloc[i][:, sl])
            mt = mk['eye_k'] * jnp.exp(gl_k[i][:, sl]) - mn[i][h][:, :kd]
            s[h] = _dot3(mt, s[h]) + mn[i][h][:, kd:]
        outs.append(jnp.concatenate(o_h, axis=1))
    for h in range(HEADS):
        s_ref[h] = s[h]
    return outs


def _gdn_dir_kernel(prev_ref, cur_ref, next_ref, cw_ref, selb_k, selg_i, selg_k, na_i, na_k, dt_i, dt_k,
                    o_ref, buf, s_ref, *, rev, nct, n_tiles, dg):
    step = pl.program_id(1)
    t = _tile_of_step(step, nct, n_tiles, rev)
    tm = TOKEN_TILE
    hl = GDN_CONV_HALO

    @pl.when(step == 0)
    def _():
        s_ref[...] = jnp.zeros_like(s_ref)

    first = jnp.logical_or(t == 0, t == nct)
    last = jnp.logical_or(t == nct - 1, t == n_tiles - 1)
    c3 = 3 * dg
    buf[0:hl] = jnp.where(first, 0.0, prev_ref[0])
    buf[hl:hl + tm] = cur_ref[0, :, :c3]
    buf[hl + tm:hl + tm + hl] = jnp.where(last, 0.0, next_ref[0])
    taps = cw_ref.shape[0]
    acc = None
    for j in range(taps):
        off = hl + j - taps // 2
        term = buf[off:off + tm] * cw_ref[j:j + 1, :]
        acc = term if acc is None else acc + term
    qkv = acc * jax.nn.sigmoid(acc)
    ones_bd = _f(_block_diag_mask(dg, dg, GDN_HEAD_DIM, GDN_HEAD_DIM)).astype(BF16)
    q = qkv[:, :dg]
    k = qkv[:, dg:2 * dg]
    v = qkv[:, 2 * dg:]
    q = q * lax.rsqrt(_dot_sum(q * q, ones_bd) + 1e-12) * (GDN_HEAD_DIM ** -0.5)
    k = k * lax.rsqrt(_dot_sum(k * k, ones_bd) + 1e-12)
    rest = cur_ref[0, :, 4 * dg:]
    beta_k = jax.nn.sigmoid(_dot_xr(rest, selb_k[...]))
    g_i = na_i[...] * _softplus(_dot_xr(rest, selg_i[...]) + dt_i[...])
    g_k = na_k[...] * _softplus(_dot_xr(rest, selg_k[...]) + dt_k[...])

    mk = _chunk_masks(rev)
    kd = GDN_HEAD_DIM
    mk['eye_k'] = _f(lax.broadcasted_iota(jnp.int32, (kd, kd), 0) == lax.broadcasted_iota(jnp.int32, (kd, kd), 1))
    mk['bd_kt'] = _f(_block_diag_mask(dg, HEADS * CHUNK, kd, CHUNK)).astype(BF16)
    mk['bd_x'] = _head_col_mask(HEADS * CHUNK, 2 * dg, CHUNK, kd)
    nch = tm // CHUNK
    order = list(range(nch - 1, -1, -1) if rev else range(nch))
    outs = _gdn_chunks(q, k, v, beta_k, g_i, g_k, s_ref, mk, order)
    for ci, o in zip(order, outs):
        o_ref[0, ci * CHUNK:(ci + 1) * CHUNK, :] = o


def _gdn_dir(p, conv_w, consts, nct, rev, dg):
    B, T, cols = p.shape
    tm = TOKEN_TILE
    n_tiles = T // tm
    hl = GDN_CONV_HALO
    hpt = tm // hl
    nhb = T // hl
    c3 = 3 * dg
    tile = lambda s: _tile_of_step(s, nct, n_tiles, rev)
    full = lambda a: pl.BlockSpec(a.shape, lambda b, s: (0,) * a.ndim)
    return pl.pallas_call(
        functools.partial(_gdn_dir_kernel, rev=rev, nct=nct, n_tiles=n_tiles, dg=dg),
        out_shape=jax.ShapeDtypeStruct((B, T, dg), F32),
        grid=(B, n_tiles),
        in_specs=[
            pl.BlockSpec((1, hl, c3), lambda b, s: (b, jnp.maximum(tile(s) * hpt - 1, 0), 0)),
            pl.BlockSpec((1, tm, cols), lambda b, s: (b, tile(s), 0)),
            pl.BlockSpec((1, hl, c3), lambda b, s: (b, jnp.minimum((tile(s) + 1) * hpt, nhb - 1), 0)),
            full(conv_w)] + [full(a) for a in consts],
        out_specs=pl.BlockSpec((1, tm, dg), lambda b, s: (b, tile(s), 0)),
        scratch_shapes=[pltpu.VMEM((tm + 2 * hl, c3), F32), pltpu.VMEM((HEADS, GDN_HEAD_DIM, GDN_HEAD_DIM), F32)],
        compiler_params=_cparams(("parallel", "arbitrary")),
        name="gdn_rev" if rev else "gdn_fwd",
    )(p, p, p, conv_w, *consts)


def _gdn_consts(a_log_d, dt_bias_d, d, dg):
    H = HEADS
    ncol = 4 * H
    def sel(base, width):
        m = np.zeros((ncol, H * width), np.float32)
        for h in range(H):
            m[base + d * H + h, h * width:(h + 1) * width] = 1.0
        return jnp.asarray(m, BF16)
    neg_a = -jnp.exp(a_log_d)
    exp_i = lambda x: jnp.repeat(x, CHUNK)[None, :]
    exp_k = lambda x: jnp.repeat(x, GDN_HEAD_DIM)[None, :]
    return (sel(0, GDN_HEAD_DIM), sel(2 * H, CHUNK), sel(2 * H, GDN_HEAD_DIM),
            exp_i(neg_a), exp_k(neg_a), exp_i(dt_bias_d), exp_k(dt_bias_d))


def _out_proj_kernel(h_ref, hy_ref, ro0_ref, ro1_ref, bv0_ref, bv1_ref, rg_ref, go0_ref, go1_ref, z_ref,
                     gnw_ref, gnb_ref, gdw_ref, w_ref, gate_ref, shift_ref, scale_ref, nw_ref, rwt_ref, rb_ref,
                     hn_ref, n_ref, lg_ref, *, splits):
    c0, c1, c2 = splits
    o = ro0_ref[0] + ro1_ref[0]
    ones_r = _f(_block_diag_mask(c1, c1, RWKV_HEAD_DIM, RWKV_HEAD_DIM)).astype(BF16)
    mean = _dot_sum(o, ones_r) * (1.0 / RWKV_HEAD_DIM)
    cen = o - mean
    var = _dot_sum(cen * cen, ones_r) * (1.0 / RWKV_HEAD_DIM)
    y_rw = (cen * lax.rsqrt(var + RWKV_GN_EPS) * gnw_ref[...] + gnb_ref[...] + bv0_ref[0] + bv1_ref[0]) * rg_ref[0]
    og = go0_ref[0] + go1_ref[0]
    ones_g = _f(_block_diag_mask(c2, c2, GDN_HEAD_DIM, GDN_HEAD_DIM)).astype(BF16)
    ms = _dot_sum(og * og, ones_g) * (1.0 / GDN_HEAD_DIM)
    z = z_ref[0]
    y_gd = og * lax.rsqrt(ms + NORM_EPS) * gdw_ref[...] * (z * jax.nn.sigmoid(z))

    acc = _dot(hy_ref[0].astype(BF16), w_ref[:c0, :])
    acc += _dot(y_rw.astype(BF16), w_ref[c0:c0 + c1, :])
    acc += _dot(y_gd.astype(BF16), w_ref[c0 + c1:c0 + c1 + c2, :])
    h = h_ref[0] + gate_ref[0, 0] * acc
    hn_ref[0] = h
    n = h * lax.rsqrt(jnp.mean(h * h, axis=-1, keepdims=True) + NORM_EPS) * nw_ref[...]
    n = n * (1.0 + scale_ref[0, 0]) + shift_ref[0, 0]
    n_ref[0] = n
    lg_ref[0] = _dot3(n, rwt_ref[...]) + rb_ref[...]


def _out_proj(h, y_hy, rw_parts, gd_parts, p_gd, gn_w, gn_b, gd_w, w_bf, gate, shift, scale, norm_w,
              router_w, router_b, n_ctx_tiles):
    B, T, D = h.shape
    tm = TOKEN_TILE
    E = router_w.shape[1]
    splits = (y_hy.shape[-1], rw_parts[0].shape[-1], gd_parts[0].shape[-1])
    c2 = splits[2]
    seg = lambda b, i: (b, (i >= n_ctx_tiles).astype(jnp.int32), 0, 0)
    tok = lambda c: pl.BlockSpec((1, tm, c), lambda b, i: (b, i, 0))
    full = lambda s: pl.BlockSpec(s, lambda b, i: (0,) * len(s))
    z_spec = pl.BlockSpec((1, tm, c2), lambda b, i: (b, i, 3))
    return pl.pallas_call(
        functools.partial(_out_proj_kernel, splits=splits),
        out_shape=[jax.ShapeDtypeStruct((B, T, D), F32), jax.ShapeDtypeStruct((B, T, D), F32),
                   jax.ShapeDtypeStruct((B, T, E), F32)],
        grid=(B, T // tm),
        in_specs=[tok(D), tok(splits[0])] + [tok(splits[1])] * 5 + [tok(c2), tok(c2), z_spec,
                  full((1, splits[1])), full((1, splits[1])), full((1, c2)), full(w_bf.shape),
                  pl.BlockSpec((1, 1, 1, D), seg), pl.BlockSpec((1, 1, 1, D), seg),
                  pl.BlockSpec((1, 1, 1, D), seg), full((1, D)), full(router_w.shape), full((1, E))],
        out_specs=[tok(D), tok(D), tok(E)],
        compiler_params=_cparams(("parallel", "parallel")),
        name="out_proj",
    )(h, y_hy, *rw_parts, *gd_parts, p_gd, gn_w, gn_b, gd_w, w_bf, gate, shift, scale, norm_w, router_w, router_b)


def _short_conv(u, w, b):
    L = u.shape[0]
    row = lax.broadcasted_iota(jnp.int32, (L, 1), 0)
    prev = jnp.where(row == 0, 0.0, pltpu.roll(u, 1, axis=0))
    nxt = jnp.where(row == L - 1, 0.0, pltpu.roll(u, L - 1, axis=0))
    return prev * w[0:1] + u * w[1:2] + nxt * w[2:3] + b


def _hyena_segment(u_ref, rows, sw_ref, sb_ref, skip_ref, cm_ref, sm_ref, p_ref, q_ref, pn_ref, d):
    conv = lambda g: _short_conv(u_ref[0, rows, g * d:(g + 1) * d], sw_ref[:, g * d:(g + 1) * d],
                                 sb_ref[:, g * d:(g + 1) * d])
    v = conv(2) * conv(1)
    vb = v.astype(BF16)
    a = _dot(cm_ref[...], vb)
    b = _dot(sm_ref[...], vb)
    p = p_ref[...]
    q = q_ref[...]
    yr = (a * p - b * q).astype(BF16)
    yi = (a * q + b * p).astype(BF16)
    y = _dot(cm_ref[...], yr) + _dot(sm_ref[...], yi)
    L = v.shape[0]
    sgn = jnp.where((lax.broadcasted_iota(jnp.int32, (L, 1), 0) & 1) == 0, 1.0, -1.0)
    a_nyq = jnp.sum(v * sgn, axis=0, keepdims=True)
    y = y + sgn * (a_nyq * pn_ref[...])
    return (y + v * skip_ref[...]) * conv(0)


def _hyena_kernel(u_ref, sw_ref, sb_ref, skip_ref, cm_l, sm_l, p_l, q_l, pn_l, cm_c, sm_c, p_c, q_c, pn_c,
                  y_ref, *, lc, with_ctx, d):
    T = u_ref.shape[1]
    lat = slice(lc, T)
    ctx = slice(0, lc)
    y_ref[0, lat, :] = _hyena_segment(u_ref, lat, sw_ref, sb_ref, skip_ref, cm_l, sm_l, p_l, q_l, pn_l, d)
    if with_ctx:
        y_ref[0, ctx, :] = _hyena_segment(u_ref, ctx, sw_ref, sb_ref, skip_ref, cm_c, sm_c, p_c, q_c, pn_c, d)
    else:
        y_ref[0, ctx, :] = jnp.zeros((lc, d), F32)


def _hyena(u, short_w, short_b, skip, lat_consts, ctx_consts, lc, with_ctx):
    B, T, cols = u.shape
    d = cols // 3
    one = pl.Buffered(1)
    full = lambda a: pl.BlockSpec(a.shape, lambda b: (0,) * a.ndim, pipeline_mode=one)
    consts = (short_w, short_b, skip) + tuple(lat_consts) + tuple(ctx_consts)
    return pl.pallas_call(
        functools.partial(_hyena_kernel, lc=lc, with_ctx=with_ctx, d=d),
        out_shape=jax.ShapeDtypeStruct((B, T, d), F32),
        grid=(B,),
        in_specs=[pl.BlockSpec((1, T, cols), lambda b: (b, 0, 0), pipeline_mode=one)] + [full(a) for a in consts],
        out_specs=pl.BlockSpec((1, T, d), lambda b: (b, 0, 0)),
        compiler_params=_cparams(("parallel",)),
        name="hyena",
    )(u, *consts)


def _moe_kernel(be_ref, nu_ref, x_ref, wgu_ref, bgu_ref, wdn_ref, bdn_ref, y_ref, wgu_bf, wdn_bf, *, d_expert):
    i = pl.program_id(0)
    prev = be_ref[jnp.maximum(i - 1, 0)]
    changed = jnp.logical_or(i == 0, be_ref[i] != prev)

    @pl.when(changed)
    def _():
        wgu_bf[...] = wgu_ref[0, 0].astype(BF16)
        wdn_bf[...] = wdn_ref[0, 0].astype(BF16)

    @pl.when(i < nu_ref[0])
    def _():
        gu = _dot(x_ref[...].astype(BF16), wgu_bf[...]) + bgu_ref[0, 0]
        gate = jnp.minimum(gu[:, :d_expert], SWIGLU_LIMIT)
        up = jnp.clip(gu[:, d_expert:], -SWIGLU_LIMIT, SWIGLU_LIMIT)
        glu = gate * jax.nn.sigmoid(gate * SWIGLU_ALPHA)
        act = ((up + 1.0) * glu).astype(BF16)
        y_ref[...] = _dot(act, wdn_bf[...]) + bdn_ref[0, 0]

    @pl.when(i >= nu_ref[0])
    def _():
        y_ref[...] = jnp.zeros_like(y_ref)


def _moe_experts(xs, block_e, n_used, w_gu, b_gu, w_dn, b_dn, layer):
    NP, D = xs.shape
    depth, E, _, F2 = w_gu.shape
    Fe = F2 // 2
    tm = MOE_TILE
    nb = NP // tm
    grid_spec = pltpu.PrefetchScalarGridSpec(
        num_scalar_prefetch=2,
        grid=(nb,),
        in_specs=[
            pl.BlockSpec((tm, D), lambda i, be, nu: (i, 0)),
            pl.BlockSpec((1, 1, D, F2), lambda i, be, nu: (layer, be[i], 0, 0)),
            pl.BlockSpec((1, 1, 1, F2), lambda i, be, nu: (layer, be[i], 0, 0)),
            pl.BlockSpec((1, 1, Fe, D), lambda i, be, nu: (layer, be[i], 0, 0)),
            pl.BlockSpec((1, 1, 1, D), lambda i, be, nu: (layer, be[i], 0, 0)),
        ],
        out_specs=pl.BlockSpec((tm, D), lambda i, be, nu: (i, 0)),
        scratch_shapes=[pltpu.VMEM((D, F2), BF16), pltpu.VMEM((Fe, D), BF16)],
    )
    return pl.pallas_call(
        functools.partial(_moe_kernel, d_expert=Fe),
        out_shape=jax.ShapeDtypeStruct((NP, D), F32),
        grid_spec=grid_spec,
        compiler_params=_cparams(("arbitrary",)),
        name="moe_experts",
    )(block_e, n_used, xs, w_gu, b_gu.reshape(depth, E, 1, F2), w_dn, b_dn.reshape(depth, E, 1, D))


def _combine_kernel(h_ref, yk_ref, rg_ref, g_ref, w_ref, o_ref, *, final):
    rg = rg_ref[0]
    y = yk_ref[0, 0].astype(F32) * rg[:, 0:1]
    for k in range(1, TOP_K):
        y += yk_ref[k, 0].astype(F32) * rg[:, k:k + 1]
    h = h_ref[0] + g_ref[0, 0] * y
    if final:
        h = h * lax.rsqrt(jnp.mean(h * h, axis=-1, keepdims=True) + NORM_EPS) * w_ref[...]
    o_ref[0] = h


def _combine(h, yk, route_gates, gate, w, n_ctx_tiles, final):
    B, T, D = h.shape
    tm = TOKEN_TILE
    tok = lambda c: pl.BlockSpec((1, tm, c), lambda b, i: (b, i, 0))
    yk_spec = pl.BlockSpec((TOP_K, 1, tm, D), lambda b, i: (0, b, i, 0))
    nseg = gate.shape[1]
    seg = lambda b, i: (b, jnp.minimum((i >= n_ctx_tiles).astype(jnp.int32), nseg - 1), 0, 0)
    return pl.pallas_call(
        functools.partial(_combine_kernel, final=final),
        out_shape=jax.ShapeDtypeStruct((B, T, D), F32),
        grid=(B, T // tm),
        in_specs=[tok(D), yk_spec, tok(TOP_K), pl.BlockSpec((1, 1, 1, D), seg),
                  pl.BlockSpec((1, D), lambda b, i: (0, 0))],
        out_specs=tok(D),
        compiler_params=_cparams(("parallel", "parallel")),
        name="combine",
    )(h, yk, route_gates, gate, w)


def _dft_mats(L):
    n2 = 2 * L
    f = lax.broadcasted_iota(jnp.int32, (L, L), 0)
    t = lax.broadcasted_iota(jnp.int32, (L, L), 1)
    ang = ((f * t) % n2).astype(F32) * (2.0 * math.pi / n2)
    return jnp.cos(ang), jnp.sin(ang)


def _hyena_filter(L, w1, b1, w2, b2, w3, b3, w4, freq, d_hy):
    pos = jnp.arange(L, dtype=F32)
    t = jnp.linspace(0.0, 1.0, L, dtype=F32)[:, None]
    bands = jnp.linspace(1e-4, HYENA_BANDS - 1, HYENA_BANDS, dtype=F32)
    ang = (2.0 * math.pi / L) * pos[:, None] * bands
    z = jnp.concatenate([t, jnp.cos(ang), -jnp.sin(ang)], axis=-1)
    hdn = jnp.sin(freq * (jnp.dot(z, w1, precision=_HI) + b1))
    hdn = jnp.sin(freq * (jnp.dot(hdn, w2, precision=_HI) + b2))
    hdn = jnp.sin(freq * (jnp.dot(hdn, w3, precision=_HI) + b3))
    h = jnp.dot(hdn, w4, precision=_HI)
    deltas = jnp.linspace(math.log(HYENA_DECAY_TARGET) / HYENA_SLOW_PCT,
                          math.log(HYENA_DECAY_TARGET) / HYENA_FAST_PCT, d_hy, dtype=F32)
    window = jnp.exp(-t * jnp.abs(deltas))
    h_fwd = h[:, :d_hy] * window
    h_bwd = h[:, d_hy:] * window
    k_lo = h_fwd
    k_hi = jnp.concatenate([jnp.zeros((1, d_hy), F32), h_bwd[:0:-1]], axis=0)
    return k_lo, k_hi


def _hyena_consts(dft, fparams, d_hy):
    cm, sm = dft
    L = cm.shape[0]
    k_lo, k_hi = _hyena_filter(L, *fparams, d_hy)
    sgn = jnp.where(jnp.arange(L) % 2 == 0, 1.0, -1.0).astype(F32)[:, None]
    wf = jnp.where(jnp.arange(L) == 0, 1.0, 2.0).astype(F32)[:, None] / (2 * L)
    cb, sb = cm.astype(BF16), sm.astype(BF16)
    klo, khi = k_lo.astype(BF16), k_hi.astype(BF16)
    p = (_dot(cb, klo) + sgn * _dot(cb, khi)) * wf
    q = (_dot(sb, klo) + sgn * _dot(sb, khi)) * wf
    pn = jnp.sum(sgn * (k_lo + k_hi), axis=0, keepdims=True) / (2 * L)
    return cm.astype(BF16), sm.astype(BF16), p, q, pn


def _pad_rows(w, lo, total):
    return jnp.zeros((total, w.shape[1]), w.dtype).at[lo:lo + w.shape[0]].set(w)


def _rwkv_mixer_parts(p, mu, w0, w2, a0, a2, g2, k_k, k_a, r_k, nct):
    row = lambda x: x[None, :]
    lw_w, lw_a = w2.shape[1], a2.shape[1]
    outs = []
    for d in range(2):
        w2p = _pad_rows(w2[d], d * lw_w, 2 * lw_w)
        a2p = _pad_rows(a2[d], d * lw_a, 2 * lw_a)
        outs.append(_rwkv_dir(p, row(mu), w2p, row(w0[d]), a2p, row(a0[d]), g2, row(k_k), row(k_a), row(r_k),
                              nct, rev=bool(d)))
    (o0, bv0, g), (o1, bv1) = outs
    return o0, o1, bv0, bv1, g


def _gdn_mixer_parts(p, conv_w, a_log, dt_bias, nct, dg):
    return [_gdn_dir(p, conv_w, _gdn_consts(a_log[d], dt_bias[d], d, dg), nct, rev=bool(d), dg=dg) for d in range(2)]


def _moe(n_bf, logits, w_gu, b_gu, w_dn, b_dn, layer):
    Nt, D = n_bf.shape
    E = w_gu.shape[1]
    tm = MOE_TILE
    top_logit, top_idx = lax.top_k(logits, TOP_K)
    gates = jax.nn.softmax(top_logit, axis=-1)
    A = Nt * TOP_K
    flat_e = top_idx.T.reshape(A)
    experts = jnp.arange(E, dtype=jnp.int32)
    counts = jnp.sum((flat_e[:, None] == experts[None, :]).astype(jnp.int32), axis=0)
    blocks_per = (counts + tm - 1) // tm
    block_end = jnp.cumsum(blocks_per)
    n_blocks = -(-A // tm) + E
    n_slots = n_blocks * tm
    block_e = jnp.minimum(jnp.sum((block_end[None, :] <= jnp.arange(n_blocks)[:, None]).astype(jnp.int32), axis=1),
                          E - 1).astype(jnp.int32)
    n_used = block_end[-1:].astype(jnp.int32)
    need = blocks_per * tm - counts
    pad_keys = jnp.where(jnp.arange(tm - 1)[None, :] < need[:, None], experts[:, None], E).reshape(E * (tm - 1))
    tail = jnp.full((n_slots - A - E * (tm - 1),), E, jnp.int32)
    keys = jnp.concatenate([flat_e, pad_keys, tail])
    ids = jnp.arange(n_slots, dtype=jnp.int32)
    _, slot_id = lax.sort((keys, ids), num_keys=1)
    _, slot_of_id = lax.sort((slot_id, ids), num_keys=1)
    dest = slot_of_id[:A]
    slot_tok = slot_id % Nt
    xs = n_bf[slot_tok]
    yb = _moe_experts(xs, block_e, n_used, w_gu, b_gu, w_dn, b_dn, layer)
    return yb[dest.reshape(TOP_K, Nt)], gates


def kernel(x, c, ctx, c_ctx, ada_w, ada_b, norm_mix_w, norm_ffn_w, final_norm_w, w_in, w_out, hy_short_w, hy_short_b, hy_f_w1, hy_f_b1, hy_f_w2, hy_f_b2, hy_f_w3, hy_f_b3, hy_f_w4, hy_f_freq, hy_skip, rw_mu, rw_w0, rw_w2, rw_a0, rw_a2, rw_g2, rw_k_k, rw_k_a, rw_r_k, rw_gn_w, rw_gn_b, gdn_conv_w, gdn_a_log, gdn_dt_bias, gdn_norm_w, moe_router_w, moe_router_b, moe_w_gu, moe_b_gu, moe_w_dn, moe_b_dn):
    B, L, D = x.shape
    Lc = ctx.shape[1]
    depth = ada_w.shape[0]
    d_hy = hy_skip.shape[-1]
    hy_cols = 3 * d_hy
    rw_cols = rw_mu.shape[-1]
    gd_cols = w_in.shape[-1] - hy_cols - rw_cols
    dg = gdn_conv_w.shape[-1] // 3
    assert gdn_a_log.shape[-1] == HEADS and rw_w0.shape[-1] == HEADS * RWKV_HEAD_DIM and dg == HEADS * GDN_HEAD_DIM
    assert Lc % TOKEN_TILE == 0 and L % TOKEN_TILE == 0 and TOKEN_TILE % GRID_W == 0
    splits = (hy_cols, rw_cols, gd_cols)
    nct = Lc // TOKEN_TILE
    row = lambda v: v[None, :]

    h = jnp.concatenate([ctx, x], axis=1)
    s_lat = jax.nn.silu(c)
    s_ctx = jax.nn.silu(c_ctx)
    dft_lat = _dft_mats(L)
    dft_ctx = _dft_mats(Lc)
    mods = None
    for l in range(depth):
        last = l == depth - 1
        ml = jnp.dot(s_lat, ada_w[l], precision=_HI) + ada_b[l]
        mc = jnp.dot(s_ctx, ada_w[l], precision=_HI) + ada_b[l]
        mods = jnp.stack([jnp.broadcast_to(mc, ml.shape), ml], axis=1).reshape(B, 2, N_MOD, 1, D)
        mod = lambda j: mods[:, :, j]
        p_hy, p_rw, p_gd = _in_proj(h, mod(0), mod(1), row(norm_mix_w[l]), w_in[l].astype(BF16), splits, nct)

        fparams = (hy_f_w1[l], hy_f_b1[l], hy_f_w2[l], hy_f_b2[l], hy_f_w3[l], hy_f_b3[l], hy_f_w4[l], hy_f_freq[l])
        y_hy = _hyena(p_hy, hy_short_w[l], row(hy_short_b[l]), row(hy_skip[l]), _hyena_consts(dft_lat, fparams, d_hy),
                      _hyena_consts(dft_ctx, fparams, d_hy), Lc, with_ctx=not last)
        rw_parts = _rwkv_mixer_parts(p_rw, rw_mu[l], rw_w0[l], rw_w2[l], rw_a0[l], rw_a2[l], rw_g2[l],
                                     rw_k_k[l], rw_k_a[l], rw_r_k[l], nct)
        gd_parts = _gdn_mixer_parts(p_gd, gdn_conv_w[l], gdn_a_log[l], gdn_dt_bias[l], nct, dg)

        h, n_bf, logits = _out_proj(h, y_hy, rw_parts, gd_parts, p_gd, row(rw_gn_w[l]), row(rw_gn_b[l]),
                                    row(jnp.tile(gdn_norm_w[l], HEADS)), w_out[l].astype(BF16), mod(2), mod(3),
                                    mod(4), row(norm_ffn_w[l]), moe_router_w[l], row(moe_router_b[l]), nct)
        if last:
            h, n_bf, logits = h[:, Lc:], n_bf[:, Lc:], logits[:, Lc:]
        Tm = h.shape[1]
        yk, route_gates = _moe(n_bf.reshape(B * Tm, D), logits.reshape(B * Tm, -1),
                               moe_w_gu, moe_b_gu, moe_w_dn, moe_b_dn, l)
        h = _combine(h, yk.reshape(TOP_K, B, Tm, D), route_gates.reshape(B, Tm, TOP_K),
                     mods[:, 1:2, 5] if last else mod(5), row(final_norm_w), 0 if last else nct, final=last)
    return h
```
